```python
import functools
import numpy as np
import jax
import jax.numpy as jnp
from jax import lax

D_MODEL = 2048
BATCH = 4
SEQ = 2048
DEPTH = 1
DEC_BATCH = 32
DEC_SEQ = 8
PAST_LEN = 8192
PAGE_SIZE = 128

ATTN_HEADS = 16
ATTN_HEAD_DIM = 128
ATTN_WIDTH = ATTN_HEADS * ATTN_HEAD_DIM
MOBA_BLOCK = 256
MOBA_TOPK = 3
MOBA_Q_CHUNK = 16
GLA_HEADS = 4
GLA_KEY_DIM = D_MODEL // (2 * GLA_HEADS)
GLA_VAL_DIM = D_MODEL // GLA_HEADS
GLA_KEY_WIDTH = GLA_HEADS * GLA_KEY_DIM
GLA_VAL_WIDTH = GLA_HEADS * GLA_VAL_DIM
GLA_GATE_RANK = 16
GLA_GATE_NORMALIZER = 16.0
GLA_CHUNK = 16
D_FF = 4 * D_MODEL
NORM_EPS = 1e-6
POOL_EXTRA_DEN = 4
IN_SPLITS = (ATTN_WIDTH, ATTN_WIDTH, ATTN_WIDTH, GLA_KEY_WIDTH, GLA_KEY_WIDTH, GLA_VAL_WIDTH, GLA_VAL_WIDTH, GLA_GATE_RANK, D_MODEL, D_MODEL)
IN_WIDTH = 3 * ATTN_WIDTH + 2 * GLA_KEY_WIDTH + 2 * GLA_VAL_WIDTH + GLA_GATE_RANK + 2 * D_MODEL

kernel_name = 'moba_gla_parallel_hybrid_step'


def rms_norm(x, g):
    xf = x.astype(jnp.float32)
    y = xf * lax.rsqrt(jnp.mean(jnp.square(xf), axis=-1, keepdims=True) + NORM_EPS)
    return (y * g.astype(jnp.float32)).astype(x.dtype)


def project_inputs(x, norm1_g, w_in, q_norm_g, k_norm_g, w_a2, b_a):
    b, s, _ = x.shape
    h = rms_norm(x, norm1_g)
    u = jnp.einsum('bsd,de->bse', h, w_in)
    qa, ka, va, qg, kg, vg, og, ag, gate_a, gate_g = jnp.split(u, np.cumsum(IN_SPLITS)[:-1].tolist(), axis=-1)
    heads = lambda t, n, d: t.reshape(b, s, n, d)
    qa = rms_norm(heads(qa, ATTN_HEADS, ATTN_HEAD_DIM), q_norm_g)
    ka = rms_norm(heads(ka, ATTN_HEADS, ATTN_HEAD_DIM), k_norm_g)
    va = heads(va, ATTN_HEADS, ATTN_HEAD_DIM)
    qg = heads(qg, GLA_HEADS, GLA_KEY_DIM) * (GLA_KEY_DIM ** -0.5)
    kg = heads(kg, GLA_HEADS, GLA_KEY_DIM)
    vg = heads(vg, GLA_HEADS, GLA_VAL_DIM)
    og = heads(og, GLA_HEADS, GLA_VAL_DIM)
    log_a = jax.nn.log_sigmoid((ag @ w_a2 + b_a).astype(jnp.float32)) / GLA_GATE_NORMALIZER
    log_a = heads(log_a, GLA_HEADS, GLA_KEY_DIM)
    return qa, ka, va, qg, kg, vg, og, log_a, gate_a, gate_g


def gla_scan(q, k, v, log_a, state0):
    b, s = q.shape[:2]
    nc = -(-s // GLA_CHUNK)
    pad = nc * GLA_CHUNK - s

    def to_chunks(t):
        t = jnp.pad(t.astype(jnp.float32), ((0, 0), (0, pad), (0, 0), (0, 0)))
        return t.reshape((b, nc, GLA_CHUNK) + t.shape[2:]).swapaxes(0, 1)

    causal = jnp.tril(jnp.ones((GLA_CHUNK, GLA_CHUNK), dtype=bool))[None, :, :, None, None]

    def step(state, inp):
        q_c, k_c, v_c, a_c = inp
        cum = jnp.cumsum(a_c, axis=1)
        decay = jnp.exp(jnp.where(causal, cum[:, :, None] - cum[:, None, :], -jnp.inf))
        scores = jnp.einsum('bthd,bshd,btshd->bhts', q_c, k_c, decay)
        o = (jnp.einsum('bthd,bhdv->bthv', q_c * jnp.exp(cum), state)
             + jnp.einsum('bhts,bshv->bthv', scores, v_c))
        last = cum[:, -1]
        state = (jnp.exp(last)[..., None] * state
                 + jnp.einsum('bshd,bshv->bhdv', k_c * jnp.exp(last[:, None] - cum), v_c))
        return state, o

    state, o = lax.scan(step, state0.astype(jnp.float32), tuple(map(to_chunks, (q, k, v, log_a))))
    o = o.swapaxes(0, 1).reshape((b, nc * GLA_CHUNK) + o.shape[3:])[:, :s]
    return o, state


def gla_branch(qg, kg, vg, og, log_a, state0, gla_norm_g):
    o, state = gla_scan(qg, kg, vg, log_a, state0)
    b, s = o.shape[:2]
    y = rms_norm(o, gla_norm_g) * jax.nn.silu(og.astype(jnp.float32))
    return y.reshape(b, s, GLA_VAL_WIDTH).astype(qg.dtype), state.astype(state0.dtype)


def select_blocks(q, k_mean, n_sel, valid=None):
    s = jnp.einsum('bthd,bnhd->bthn', q, k_mean).astype(jnp.float32)
    if valid is not None:
        s = jnp.where(valid[None, :, None, :], s, -jnp.inf)
    return lax.top_k(s, n_sel)[1]


def moba_core(q, k_own, v_own, own_mask, k_sel=None, v_sel=None, sel_valid=None):
    scale = ATTN_HEAD_DIM ** -0.5
    own = jnp.einsum('bqhd,bhnd->bqhn', q, k_own).astype(jnp.float32) * scale
    own = jnp.where(own_mask[None, :, None, :], own, -jnp.inf)
    if k_sel is None:
        p = jax.nn.softmax(own, axis=-1).astype(v_own.dtype)
        return jnp.einsum('bqhn,bhnd->bqhd', p, v_own)
    bq, nq, h, n_k, n_b, _ = k_sel.shape
    sel = jnp.einsum('bqhd,bqhknd->bqhkn', q, k_sel).astype(jnp.float32) * scale
    if sel_valid is not None:
        sel = jnp.where(sel_valid[..., None], sel, -jnp.inf)
    logits = jnp.concatenate([sel.reshape(bq, nq, h, n_k * n_b), own], axis=-1)
    p = jax.nn.softmax(logits, axis=-1).astype(v_own.dtype)
    p_sel = p[..., :n_k * n_b].reshape(bq, nq, h, n_k, n_b)
    return (jnp.einsum('bqhkn,bqhknd->bqhd', p_sel, v_sel)
            + jnp.einsum('bqhn,bhnd->bqhd', p[..., n_k * n_b:], v_own))


def moba_prompt(q, k, v):
    b, s, h, d = q.shape
    nb = -(-s // MOBA_BLOCK)
    pad = nb * MOBA_BLOCK - s

    def blocks(t):
        t = jnp.pad(t, ((0, 0), (0, pad), (0, 0), (0, 0))).transpose(0, 2, 1, 3)
        return t.reshape(b, h, nb, MOBA_BLOCK, d)

    kb, vb = blocks(k), blocks(v)
    n_sel = min(MOBA_TOPK, nb - 1)
    nq = s // MOBA_Q_CHUNK
    starts = jnp.arange(nq, dtype=jnp.int32) * MOBA_Q_CHUNK
    to_q_chunks = lambda t: t.reshape((b, nq, MOBA_Q_CHUNK) + t.shape[2:]).swapaxes(0, 1)
    xs = (to_q_chunks(q), starts)
    if n_sel > 0:
        cur = jnp.arange(s) // MOBA_BLOCK
        k_mean = (jnp.sum(kb, axis=3, dtype=jnp.float32) / MOBA_BLOCK).astype(q.dtype).transpose(0, 2, 1, 3)
        idx = select_blocks(q, k_mean, n_sel, jnp.arange(nb)[None, :] < cur[:, None])
        valid = idx < cur[None, :, None, None]
        xs = xs + (to_q_chunks(idx), to_q_chunks(valid))
    bi = jnp.arange(b)[:, None, None, None]
    hi = jnp.arange(h)[None, None, :, None]
    q_off = jnp.arange(MOBA_Q_CHUNK)
    key_off = jnp.arange(MOBA_BLOCK)

    def chunk(args):
        q_c, start = args[0], args[1]
        blk = start // MOBA_BLOCK
        k_own = lax.dynamic_index_in_dim(kb, blk, axis=2, keepdims=False)
        v_own = lax.dynamic_index_in_dim(vb, blk, axis=2, keepdims=False)
        own_mask = (blk * MOBA_BLOCK + key_off)[None, :] <= (start + q_off)[:, None]
        if n_sel > 0:
            idx_c, valid_c = args[2], args[3]
            return moba_core(q_c, k_own, v_own, own_mask, kb[bi, hi, idx_c], vb[bi, hi, idx_c], valid_c)
        return moba_core(q_c, k_own, v_own, own_mask)

    out = lax.map(chunk, xs)
    return out.swapaxes(0, 1).reshape(b, s, h * d)


def moba_sample(q, k_new, v_new, *, cache_k, cache_v, page_sum_k, layer, page_table):
    db, t, h, d = q.shape
    past = page_table.shape[1] * PAGE_SIZE
    ppb = MOBA_BLOCK // PAGE_SIZE
    cb = past // MOBA_BLOCK
    assert past % MOBA_BLOCK + t <= MOBA_BLOCK
    own_pages = page_table[:, cb * ppb:]
    n_own = own_pages.shape[1] * PAGE_SIZE

    def own_rows(cache, new):
        rows = cache[layer, own_pages].transpose(0, 2, 1, 3, 4).reshape(db, h, n_own, d)
        return jnp.concatenate([rows, new.transpose(0, 2, 1, 3)], axis=2)

    k_own, v_own = own_rows(cache_k, k_new), own_rows(cache_v, v_new)
    own_mask = jnp.concatenate([jnp.ones((t, n_own), dtype=bool), jnp.tril(jnp.ones((t, t), dtype=bool))], axis=1)
    n_sel = min(MOBA_TOPK, cb)
    if n_sel == 0:
        return moba_core(q, k_own, v_own, own_mask).reshape(db, t, h * d)
    blk_sum = page_sum_k[page_table[:, :cb * ppb]].reshape(db, cb, ppb, h, d).sum(axis=2)
    k_mean = (blk_sum / MOBA_BLOCK).astype(q.dtype)
    idx = select_blocks(q, k_mean, n_sel)
    pages = idx[..., None] * ppb + jnp.arange(ppb)
    phys = page_table[jnp.arange(db)[:, None, None, None, None], pages]
    hi = jnp.arange(h)[None, :, None, None]

    def token(args):
        q_t, phys_t, mask_t = args
        gather = lambda cache: cache[layer, phys_t, hi].reshape(db, 1, h, n_sel, MOBA_BLOCK, d)
        return moba_core(q_t[:, None], k_own, v_own, mask_t[None], gather(cache_k), gather(cache_v))[:, 0]

    out = lax.map(token, (q.swapaxes(0, 1), phys.swapaxes(0, 1), own_mask))
    return out.swapaxes(0, 1).reshape(db, t, h * d)


def run_layer(x, attend, gla_state0, norm1_g, w_in, q_norm_g, k_norm_g, w_a2, b_a, gla_norm_g,
              w_branch_attn, w_branch_gla, w_out, norm2_g, w_up, w_down):
    qa, ka, va, qg, kg, vg, og, log_a, gate_a, gate_g = project_inputs(x, norm1_g, w_in, q_norm_g, k_norm_g, w_a2, b_a)
    y_attn = attend(qa, ka, va)
    y_gla, gla_state = gla_branch(qg, kg, vg, og, log_a, gla_state0, gla_norm_g)
    mixed = (jax.nn.sigmoid(gate_a) * (y_attn @ w_branch_attn)
             + jax.nn.sigmoid(gate_g) * (y_gla @ w_branch_gla))
    x = x + mixed @ w_out
    x = x + jnp.square(jax.nn.relu(rms_norm(x, norm2_g) @ w_up)) @ w_down
    return x, ka.transpose(0, 2, 1, 3), va.transpose(0, 2, 1, 3), gla_state


def setup_inputs(seed: int = 0) -> dict:
    key = jax.random.key(seed)
    ks = jax.random.split(key, 20)
    n_pages = PAST_LEN // PAGE_SIZE
    n_used = DEC_BATCH * n_pages
    n_pool = n_used + n_used // POOL_EXTRA_DEN
    f32 = jnp.float32
    nrm = lambda k, shape: jax.random.normal(k, shape, f32)
    w = lambda k, shape, fan_in: nrm(k, shape) * fan_in ** -0.5
    gain = lambda k, shape: 1.0 + 0.02 * nrm(k, shape)
    page_table = jax.random.permutation(ks[5], n_pool)[:n_used].reshape(DEC_BATCH, n_pages).astype(jnp.int32)
    return {
        'x_prompt': nrm(ks[0], (BATCH, SEQ, D_MODEL)),
        'x_sample': nrm(ks[1], (DEC_BATCH, DEC_SEQ, D_MODEL)),
        'cache_k': nrm(ks[2], (DEPTH, n_pool, ATTN_HEADS, PAGE_SIZE, ATTN_HEAD_DIM)),
        'cache_v': nrm(ks[3], (DEPTH, n_pool, ATTN_HEADS, PAGE_SIZE, ATTN_HEAD_DIM)),
        'state_gla': nrm(ks[4], (DEPTH, DEC_BATCH, GLA_HEADS, GLA_KEY_DIM, GLA_VAL_DIM)),
        'page_table': page_table,
        'norm1_g': gain(ks[6], (DEPTH, D_MODEL)),
        'w_in': w(ks[7], (DEPTH, D_MODEL, IN_WIDTH), D_MODEL),
        'q_norm_g': gain(ks[8], (DEPTH, ATTN_HEAD_DIM)),
        'k_norm_g': gain(ks[9], (DEPTH, ATTN_HEAD_DIM)),
        'w_a2': w(ks[10], (DEPTH, GLA_GATE_RANK, GLA_KEY_WIDTH), GLA_GATE_RANK),
        'b_a': 0.1 * nrm(ks[11], (DEPTH, GLA_KEY_WIDTH)),
        'gla_norm_g': gain(ks[12], (DEPTH, GLA_VAL_DIM)),
        'w_branch_attn': w(ks[13], (DEPTH, ATTN_WIDTH, D_MODEL), ATTN_WIDTH),
        'w_branch_gla': w(ks[14], (DEPTH, GLA_VAL_WIDTH, D_MODEL), GLA_VAL_WIDTH),
        'w_out': w(ks[15], (DEPTH, D_MODEL, D_MODEL), D_MODEL),
        'norm2_g': gain(ks[16], (DEPTH, D_MODEL)),
        'w_up': w(ks[17], (DEPTH, D_MODEL, D_FF), D_MODEL),
        'w_down': w(ks[18], (DEPTH, D_FF, D_MODEL), D_FF),
    }


def reference(x_prompt, x_sample, cache_k, cache_v, state_gla, page_table, norm1_g, w_in, q_norm_g, k_norm_g,
              w_a2, b_a, gla_norm_g, w_branch_attn, w_branch_gla, w_out, norm2_g, w_up, w_down):
    page_sums = jnp.sum(cache_k, axis=3, dtype=jnp.float32)
    prompt_state0 = jnp.zeros((x_prompt.shape[0],) + state_gla.shape[2:], state_gla.dtype)
    yp, ys = x_prompt, x_sample
    kp, vp, sp, kss, vss, sss = [], [], [], [], [], []
    for l in range(DEPTH):
        wl = [p[l] for p in (norm1_g, w_in, q_norm_g, k_norm_g, w_a2, b_a, gla_norm_g,
                             w_branch_attn, w_branch_gla, w_out, norm2_g, w_up, w_down)]
        yp, k_l, v_l, s_l = run_layer(yp, moba_prompt, prompt_state0, *wl)
        kp.append(k_l); vp.append(v_l); sp.append(s_l)
        attend_s = functools.partial(moba_sample, cache_k=cache_k, cache_v=cache_v, page_sum_k=page_sums[l],
                                     layer=l, page_table=page_table)
        ys, k_l, v_l, s_l = run_layer(ys, attend_s, state_gla[l], *wl)
        kss.append(k_l); vss.append(v_l); sss.append(s_l)
    return (yp, ys, jnp.stack(kp), jnp.stack(vp), jnp.stack(sp), jnp.stack(kss), jnp.stack(vss), jnp.stack(sss))
```

```python
import functools

import numpy as np
import jax
import jax.numpy as jnp
from jax import lax
from jax.experimental import pallas as pl
from jax.experimental.pallas import tpu as pltpu

F32 = jnp.float32
BF16 = jnp.bfloat16
HIGHEST = lax.Precision.HIGHEST

NORM_EPS = 1e-6
ATTN_HEADS = 16
HEAD_DIM = 128
MOBA_BLOCK = 256
MOBA_TOPK = 3
GLA_HEADS = 4
GLA_KEY_DIM = 256
GLA_VAL_DIM = 512
GLA_GATE_RANK = 16
GLA_GATE_NORMALIZER = 16.0

LANES = 128
VMEM_LIMIT_BYTES = 56 * 1024 * 1024
GLA_CHUNK = 128
MASK_VALUE = -1e30

NT_DIMS = (((1,), (1,)), ((), ()))
TN_DIMS = (((0,), (0,)), ((), ()))


def _params(*sem):
    return pltpu.CompilerParams(dimension_semantics=sem, vmem_limit_bytes=VMEM_LIMIT_BYTES)


def _sigmoid(x):
    return 1.0 / (1.0 + jnp.exp(-x))


def _rmsnorm_kernel(x_ref, g_ref, o_ref):
    x = x_ref[...]
    ms = jnp.mean(x * x, axis=-1, keepdims=True)
    o_ref[...] = (x * lax.rsqrt(ms + NORM_EPS) * g_ref[...]).astype(o_ref.dtype)


def _rmsnorm(x, g, tm):
    m, d = x.shape
    return pl.pallas_call(
        _rmsnorm_kernel,
        grid=(m // tm,),
        in_specs=[pl.BlockSpec((tm, d), lambda i: (i, 0)), pl.BlockSpec((1, d), lambda i: (0, 0))],
        out_specs=pl.BlockSpec((tm, d), lambda i: (i, 0)),
        out_shape=jax.ShapeDtypeStruct((m, d), BF16),
        compiler_params=_params("parallel"),
        name="rmsnorm",
    )(x, g.reshape(1, d))


def _matmul_kernel(a_ref, w_ref, o_ref):
    o_ref[...] = jnp.dot(a_ref[...], w_ref[...], preferred_element_type=F32).astype(o_ref.dtype)


def _matmul(a, w, out_dtype, tm, tn, name):
    m, k = a.shape
    n = w.shape[1]
    return pl.pallas_call(
        _matmul_kernel,
        grid=(m // tm, n // tn),
        in_specs=[pl.BlockSpec((tm, k), lambda i, j: (i, 0)), pl.BlockSpec((k, tn), lambda i, j: (0, j))],
        out_specs=pl.BlockSpec((tm, tn), lambda i, j: (i, j)),
        out_shape=jax.ShapeDtypeStruct((m, n), out_dtype),
        compiler_params=_params("parallel", "parallel"),
        name=name,
    )(a, w)


def _matmul_residual_kernel(a_ref, w_ref, r_ref, o_ref):
    o_ref[...] = r_ref[...] + jnp.dot(a_ref[...], w_ref[...], preferred_element_type=F32)


def _matmul_residual(a, w, r, tm, tn, name):
    m, k = a.shape
    n = w.shape[1]
    return pl.pallas_call(
        _matmul_residual_kernel,
        grid=(m // tm, n // tn),
        in_specs=[pl.BlockSpec((tm, k), lambda i, j: (i, 0)), pl.BlockSpec((k, tn), lambda i, j: (0, j)),
                  pl.BlockSpec((tm, tn), lambda i, j: (i, j))],
        out_specs=pl.BlockSpec((tm, tn), lambda i, j: (i, j)),
        out_shape=jax.ShapeDtypeStruct((m, n), F32),
        compiler_params=_params("parallel", "parallel"),
        name=name,
    )(a, w, r)


def _proj_heads_kernel(h_ref, w_ref, g_ref, o_ref, *, normalize, heads_per_tile, seq_major):
    res = jnp.dot(h_ref[...], w_ref[...], preferred_element_type=F32)
    for i in range(heads_per_tile):
        r = res[:, i * HEAD_DIM:(i + 1) * HEAD_DIM]
        if normalize:
            ms = jnp.mean(r * r, axis=-1, keepdims=True)
            r = r * lax.rsqrt(ms + NORM_EPS) * g_ref[...]
        if seq_major:
            o_ref[0, i] = r
        else:
            o_ref[:, i] = r.reshape(o_ref.shape[0], o_ref.shape[2], HEAD_DIM)


def _proj_heads(h, w, g, nseq, seqlen, normalize, tm, heads_per_tile, name):
    m, d = h.shape
    nh = w.shape[1] // HEAD_DIM
    tn = heads_per_tile * HEAD_DIM
    out_shape = jax.ShapeDtypeStruct((nseq, nh, seqlen, HEAD_DIM), F32)
    gain = g.reshape(1, HEAD_DIM)
    if seqlen % tm == 0:
        spt = seqlen // tm
        kern = functools.partial(_proj_heads_kernel, normalize=normalize, heads_per_tile=heads_per_tile, seq_major=True)
        return pl.pallas_call(
            kern,
            grid=(nseq, spt, nh // heads_per_tile),
            in_specs=[pl.BlockSpec((tm, d), lambda b, s, j: (b * spt + s, 0)),
                      pl.BlockSpec((d, tn), lambda b, s, j: (0, j)),
                      pl.BlockSpec((1, HEAD_DIM), lambda b, s, j: (0, 0))],
            out_specs=pl.BlockSpec((1, heads_per_tile, tm, HEAD_DIM), lambda b, s, j: (b, j, s, 0)),
            out_shape=out_shape,
            compiler_params=_params("parallel", "parallel", "parallel"),
            name=name,
        )(h, w, gain)
    assert tm == m
    kern = functools.partial(_proj_heads_kernel, normalize=normalize, heads_per_tile=heads_per_tile, seq_major=False)
    return pl.pallas_call(
        kern,
        grid=(nh // heads_per_tile,),
        in_specs=[pl.BlockSpec((m, d), lambda j: (0, 0)),
                  pl.BlockSpec((d, tn), lambda j: (0, j)),
                  pl.BlockSpec((1, HEAD_DIM), lambda j: (0, 0))],
        out_specs=pl.BlockSpec((nseq, heads_per_tile, seqlen, HEAD_DIM), lambda j: (0, j, 0, 0)),
        out_shape=out_shape,
        compiler_params=_params("parallel"),
        name=name,
    )(h, w, gain)


def _loga_kernel(h_ref, wag_ref, wa2_ref, ba_ref, o_ref):
    ag = jnp.dot(h_ref[...], wag_ref[...], preferred_element_type=F32)
    z = jnp.dot(ag, wa2_ref[...], preferred_element_type=F32, precision=HIGHEST) + ba_ref[...]
    ls = jnp.minimum(z, 0.0) - jnp.log1p(jnp.exp(-jnp.abs(z)))
    o_ref[...] = ls * (1.0 / GLA_GATE_NORMALIZER)


def _loga(h, wag, wa2, ba, tm):
    m, d = h.shape
    n = wa2.shape[1]
    return pl.pallas_call(
        _loga_kernel,
        grid=(m // tm,),
        in_specs=[pl.BlockSpec((tm, d), lambda i: (i, 0)), pl.BlockSpec((d, LANES), lambda i: (0, 0)),
                  pl.BlockSpec((LANES, n), lambda i: (0, 0)), pl.BlockSpec((1, n), lambda i: (0, 0))],
        out_specs=pl.BlockSpec((tm, n), lambda i: (i, 0)),
        out_shape=jax.ShapeDtypeStruct((m, n), F32),
        compiler_params=_params("parallel"),
        name="gla_log_alpha",
    )(h, wag, wa2, ba)


def _merge_kernel(h_ref, ya_ref, yg_ref, wga_ref, wgg_ref, wba_ref, wbg_ref, o_ref):
    h = h_ref[...]
    ga = jnp.dot(h, wga_ref[...], preferred_element_type=F32)
    gg = jnp.dot(h, wgg_ref[...], preferred_element_type=F32)
    ba = jnp.dot(ya_ref[...].astype(BF16), wba_ref[...], preferred_element_type=F32)
    bg = jnp.dot(yg_ref[...].astype(BF16), wbg_ref[...], preferred_element_type=F32)
    o_ref[...] = (_sigmoid(ga) * ba + _sigmoid(gg) * bg).astype(o_ref.dtype)


def _merge(h, y_attn, y_gla, wga, wgg, wba, wbg, tm, tn):
    m, d = h.shape
    n = wga.shape[1]
    row = lambda i, j: (i, 0)
    col = lambda i, j: (0, j)
    return pl.pallas_call(
        _merge_kernel,
        grid=(m // tm, n // tn),
        in_specs=[pl.BlockSpec((tm, d), row), pl.BlockSpec((tm, y_attn.shape[1]), row),
                  pl.BlockSpec((tm, y_gla.shape[1]), row),
                  pl.BlockSpec((d, tn), col), pl.BlockSpec((d, tn), col),
                  pl.BlockSpec((wba.shape[0], tn), col), pl.BlockSpec((wbg.shape[0], tn), col)],
        out_specs=pl.BlockSpec((tm, tn), lambda i, j: (i, j)),
        out_shape=jax.ShapeDtypeStruct((m, n), BF16),
        compiler_params=_params("parallel", "parallel"),
        name="branch_merge",
    )(h, y_attn, y_gla, wga, wgg, wba, wbg)


def _mlp_kernel(x_ref, g_ref, wup_ref, wdn_ref, o_ref, h_ref, acc_ref):
    f = pl.program_id(1)

    @pl.when(f == 0)
    def _():
        x = x_ref[...]
        ms = jnp.mean(x * x, axis=-1, keepdims=True)
        h_ref[...] = (x * lax.rsqrt(ms + NORM_EPS) * g_ref[...]).astype(BF16)
        acc_ref[...] = jnp.zeros_like(acc_ref)

    u = jnp.dot(h_ref[...], wup_ref[...], preferred_element_type=F32)
    a = jnp.square(jnp.maximum(u, 0.0)).astype(BF16)
    acc_ref[...] += jnp.dot(a, wdn_ref[...], preferred_element_type=F32)

    @pl.when(f == pl.num_programs(1) - 1)
    def _():
        o_ref[...] = x_ref[...] + acc_ref[...]


def _mlp(x, g, wup, wdn, tm, tf):
    m, d = x.shape
    ff = wup.shape[1]
    return pl.pallas_call(
        _mlp_kernel,
        grid=(m // tm, ff // tf),
        in_specs=[pl.BlockSpec((tm, d), lambda i, f: (i, 0)), pl.BlockSpec((1, d), lambda i, f: (0, 0)),
                  pl.BlockSpec((d, tf), lambda i, f: (0, f)), pl.BlockSpec((tf, d), lambda i, f: (f, 0))],
        out_specs=pl.BlockSpec((tm, d), lambda i, f: (i, 0)),
        out_shape=jax.ShapeDtypeStruct((m, d), F32),
        scratch_shapes=[pltpu.VMEM((tm, d), BF16), pltpu.VMEM((tm, d), F32)],
        compiler_params=_params("parallel", "arbitrary"),
        name="mlp",
    )(x, g.reshape(1, d), wup, wdn)


def _top_blocks(scores, n_valid, n_sel):
    lane = lax.broadcasted_iota(jnp.int32, scores.shape, 1)
    lanef = lane.astype(F32)
    valid = lane < n_valid
    sv = jnp.where(valid, scores, -jnp.inf)
    sel = jnp.zeros(scores.shape, F32)
    for _ in range(n_sel):
        mx = jnp.max(sv, axis=-1, keepdims=True)
        idx = jnp.min(jnp.where(sv == mx, lanef, float(LANES)), axis=-1, keepdims=True)
        pick = lanef == idx
        sel = jnp.where(pick, 1.0, sel)
        sv = jnp.where(pick, -jnp.inf, sv)
    return jnp.where(valid, sel, 0.0)


def _moba_prompt_kernel(q_ref, k_ref, v_ref, o_ref, m_ref, l_ref, acc_ref, *, n_blocks):
    i = pl.program_id(2)
    scale = HEAD_DIM ** -0.5
    q = q_ref[0, 0]
    means = [jnp.sum(k_ref[0, 0, j * MOBA_BLOCK:(j + 1) * MOBA_BLOCK, :], axis=0, keepdims=True) / MOBA_BLOCK
             for j in range(n_blocks)]
    kmean = jnp.concatenate(means + [jnp.zeros((LANES - n_blocks, HEAD_DIM), F32)], axis=0)
    gate = lax.dot_general(q, kmean, NT_DIMS, precision=HIGHEST, preferred_element_type=F32)
    sel = _top_blocks(gate, i, min(MOBA_TOPK, n_blocks - 1))

    qb = q.astype(BF16)
    start = pl.multiple_of(i * MOBA_BLOCK, MOBA_BLOCK)
    k_own = k_ref[0, 0, pl.ds(start, MOBA_BLOCK), :].astype(BF16)
    v_own = v_ref[0, 0, pl.ds(start, MOBA_BLOCK), :].astype(BF16)
    sc = lax.dot_general(qb, k_own, NT_DIMS, preferred_element_type=F32) * scale
    row = lax.broadcasted_iota(jnp.int32, sc.shape, 0)
    col = lax.broadcasted_iota(jnp.int32, sc.shape, 1)
    sc = jnp.where(col <= row, sc, MASK_VALUE)
    m0 = jnp.max(sc, axis=-1, keepdims=True)
    p = jnp.exp(sc - m0)
    m_ref[...] = m0
    l_ref[...] = jnp.sum(p, axis=-1, keepdims=True)
    acc_ref[...] = jnp.dot(p.astype(BF16), v_own, preferred_element_type=F32)

    for j in range(n_blocks - 1):
        @pl.when(j < i)
        def _(j=j):
            k_j = k_ref[0, 0, j * MOBA_BLOCK:(j + 1) * MOBA_BLOCK, :].astype(BF16)
            v_j = v_ref[0, 0, j * MOBA_BLOCK:(j + 1) * MOBA_BLOCK, :].astype(BF16)
            s_j = lax.dot_general(qb, k_j, NT_DIMS, preferred_element_type=F32) * scale
            s_j = jnp.where(sel[:, j:j + 1] > 0.5, s_j, MASK_VALUE)
            m_old = m_ref[...]
            m_new = jnp.maximum(m_old, jnp.max(s_j, axis=-1, keepdims=True))
            alpha = jnp.exp(m_old - m_new)
            p_j = jnp.exp(s_j - m_new)
            l_ref[...] = alpha * l_ref[...] + jnp.sum(p_j, axis=-1, keepdims=True)
            acc_ref[...] = alpha * acc_ref[...] + jnp.dot(p_j.astype(BF16), v_j, preferred_element_type=F32)
            m_ref[...] = m_new

    o_ref[0] = (acc_ref[...] / l_ref[...]).astype(o_ref.dtype)


def _moba_prompt(q, k, v):
    b, h, s, d = q.shape
    assert s % MOBA_BLOCK == 0 and d == HEAD_DIM
    nb = s // MOBA_BLOCK
    kern = functools.partial(_moba_prompt_kernel, n_blocks=nb)
    return pl.pallas_call(
        kern,
        grid=(b, h, nb),
        in_specs=[pl.BlockSpec((1, 1, MOBA_BLOCK, d), lambda bi, hi, i: (bi, hi, i, 0)),
                  pl.BlockSpec((1, 1, s, d), lambda bi, hi, i: (bi, hi, 0, 0)),
                  pl.BlockSpec((1, 1, s, d), lambda bi, hi, i: (bi, hi, 0, 0))],
        out_specs=pl.BlockSpec((1, MOBA_BLOCK, d), lambda bi, hi, i: (bi, i, hi)),
        out_shape=jax.ShapeDtypeStruct((b, s, h * d), BF16),
        scratch_shapes=[pltpu.VMEM((MOBA_BLOCK, 1), F32), pltpu.VMEM((MOBA_BLOCK, 1), F32),
                        pltpu.VMEM((MOBA_BLOCK, d), F32)],
        compiler_params=_params("parallel", "parallel", "arbitrary"),
        name="moba_prompt",
    )(q, k, v)


def _moba_sample_probs_kernel(pt_ref, q_ref, kn_ref, kc_ref, p_ref, pown_ref, bsum_ref, *,
                              n_pages, pages_per_block):
    pg = pl.program_id(1)
    n_heads = q_ref.shape[1]
    n_tok = q_ref.shape[2]
    n_blocks = n_pages // pages_per_block
    scale = HEAD_DIM ** -0.5

    @pl.when(pg == 0)
    def _():
        bsum_ref[...] = jnp.zeros_like(bsum_ref)

    blk = pg // pages_per_block
    for h in range(n_heads):
        kh = kc_ref[0, 0, h]
        bsum_ref[h, pl.ds(blk, 1), :] += jnp.sum(kh, axis=0, keepdims=True)
        lg = lax.dot_general(q_ref[0, h].astype(BF16), kh.astype(BF16), NT_DIMS, preferred_element_type=F32)
        p_ref[0, pg, h] = lg * scale

    @pl.when(pg == n_pages - 1)
    def _():
        def head_body(h, carry):
            qh = q_ref[0, h]
            kmean = bsum_ref[h] / MOBA_BLOCK
            gate = lax.dot_general(qh, kmean, NT_DIMS, precision=HIGHEST, preferred_element_type=F32)
            sel = _top_blocks(gate, n_blocks, min(MOBA_TOPK, n_blocks))
            kn = jnp.concatenate([kn_ref[0, h], jnp.zeros((LANES - n_tok, HEAD_DIM), F32)], axis=0).astype(BF16)
            own = lax.dot_general(qh.astype(BF16), kn, NT_DIMS, preferred_element_type=F32) * scale
            row = lax.broadcasted_iota(jnp.int32, own.shape, 0)
            col = lax.broadcasted_iota(jnp.int32, own.shape, 1)
            own = jnp.where(col <= row, own, MASK_VALUE)
            mrun = own
            for b in range(n_blocks):
                sel_b = sel[:, b:b + 1] > 0.5
                for pp in range(pages_per_block):
                    page = b * pages_per_block + pp
                    lg = jnp.where(sel_b, p_ref[0, page, h], MASK_VALUE)
                    p_ref[0, page, h] = lg
                    mrun = jnp.maximum(mrun, lg)
            m = jnp.max(mrun, axis=-1, keepdims=True)
            e_own = jnp.exp(own - m)
            lrun = e_own
            for page in range(n_pages):
                e = jnp.exp(p_ref[0, page, h] - m)
                p_ref[0, page, h] = e
                lrun = lrun + e
            inv = 1.0 / jnp.sum(lrun, axis=-1, keepdims=True)
            for page in range(n_pages):
                p_ref[0, page, h] = p_ref[0, page, h] * inv
            pown_ref[0, h] = e_own * inv
            return carry

        lax.fori_loop(0, n_heads, head_body, 0)


def _moba_sample_values_kernel(pt_ref, p_ref, pown_ref, vn_ref, vc_ref, o_ref):
    pg = pl.program_id(1)
    n_heads = vn_ref.shape[1]
    n_tok = vn_ref.shape[2]

    @pl.when(pg == 0)
    def _():
        for h in range(n_heads):
            pown = pown_ref[0, h]
            vn = vn_ref[0, h]
            own = pown[:, 0:1] * vn[0:1, :]
            for t in range(1, n_tok):
                own = own + pown[:, t:t + 1] * vn[t:t + 1, :]
            o_ref[0, :, h * HEAD_DIM:(h + 1) * HEAD_DIM] = own

    for h in range(n_heads):
        o_ref[0, :, h * HEAD_DIM:(h + 1) * HEAD_DIM] += jnp.dot(
            p_ref[0, 0, h].astype(BF16), vc_ref[0, 0, h].astype(BF16), preferred_element_type=F32)


def _moba_sample(q, k_new, v_new, cache_k, cache_v, page_table, layer):
    db, h, t, d = q.shape
    n_pages = page_table.shape[1]
    page = cache_k.shape[3]
    ppb = MOBA_BLOCK // page
    assert page == LANES and d == HEAD_DIM and MOBA_BLOCK % page == 0
    assert (n_pages * page) % MOBA_BLOCK == 0 and t <= MOBA_BLOCK
    assert n_pages // ppb <= LANES and n_pages // ppb >= MOBA_TOPK
    pt = page_table.reshape(-1).astype(jnp.int32)

    seq4 = lambda di, pi, pt_ref: (di, 0, 0, 0)
    cache_map = lambda di, pi, pt_ref: (layer, pt_ref[di * n_pages + pi], 0, 0, 0)

    probs, p_own = pl.pallas_call(
        functools.partial(_moba_sample_probs_kernel, n_pages=n_pages, pages_per_block=ppb),
        grid_spec=pltpu.PrefetchScalarGridSpec(
            num_scalar_prefetch=1,
            grid=(db, n_pages),
            in_specs=[pl.BlockSpec((1, h, t, d), seq4), pl.BlockSpec((1, h, t, d), seq4),
                      pl.BlockSpec((1, 1, h, page, d), cache_map)],
            out_specs=[pl.BlockSpec((1, n_pages, h, t, page), lambda di, pi, pt_ref: (di, 0, 0, 0, 0)),
                       pl.BlockSpec((1, h, t, LANES), seq4)],
            scratch_shapes=[pltpu.VMEM((h, LANES, d), F32)],
        ),
        out_shape=[jax.ShapeDtypeStruct((db, n_pages, h, t, page), F32),
                   jax.ShapeDtypeStruct((db, h, t, LANES), F32)],
        compiler_params=_params("parallel", "arbitrary"),
        name="moba_sample_probs",
    )(pt, q, k_new, cache_k)

    return pl.pallas_call(
        _moba_sample_values_kernel,
        grid_spec=pltpu.PrefetchScalarGridSpec(
            num_scalar_prefetch=1,
            grid=(db, n_pages),
            in_specs=[pl.BlockSpec((1, 1, h, t, page), lambda di, pi, pt_ref: (di, pi, 0, 0, 0)),
                      pl.BlockSpec((1, h, t, LANES), seq4), pl.BlockSpec((1, h, t, d), seq4),
                      pl.BlockSpec((1, 1, h, page, d), cache_map)],
            out_specs=pl.BlockSpec((1, t, h * d), lambda di, pi, pt_ref: (di, 0, 0)),
        ),
        out_shape=jax.ShapeDtypeStruct((db, t, h * d), F32),
        compiler_params=_params("parallel", "arbitrary"),
        name="moba_sample_values",
    )(pt, probs, p_own, v_new, cache_v)


def _gla_tables(c):
    widths = []
    w = c
    while w >= 2:
        widths.append(w)
        w //= 2
    t = np.arange(c)
    prefix = [np.tril(np.ones((c, c), np.float32))]
    masks = [np.eye(c, dtype=np.float32)]
    for w in widths:
        base = (t // w) * w
        mid = base + w // 2
        upper = t >= mid
        pm = np.zeros((c, c), np.float32)
        for r in range(c):
            if upper[r]:
                pm[r, mid[r]:r + 1] = 1.0
            else:
                pm[r, r + 1:mid[r]] = 1.0
        prefix.append(pm)
        same = base[:, None] == base[None, :]
        masks.append((same & upper[:, None] & (~upper)[None, :]).astype(np.float32))
    return np.concatenate(prefix, axis=0), np.stack(masks, axis=0)


def _gla_kernel(*refs, n_rows, has_state0):
    if has_state0:
        q_ref, k_ref, v_ref, og_ref, a_ref, pre_ref, msk_ref, g_ref, s0_ref, y_ref, sout_ref, st_ref = refs
    else:
        q_ref, k_ref, v_ref, og_ref, a_ref, pre_ref, msk_ref, g_ref, y_ref, sout_ref, st_ref = refs
        s0_ref = None
    ci = pl.program_id(2)
    c = GLA_CHUNK
    n_levels = msk_ref.shape[0]

    @pl.when(ci == 0)
    def _():
        if has_state0:
            st_ref[...] = s0_ref[0, 0].T
        else:
            st_ref[...] = jnp.zeros_like(st_ref)

    def rows(ref):
        x = ref[...]
        if n_rows < c:
            x = jnp.concatenate([x, jnp.zeros((c - n_rows, x.shape[1]), x.dtype)], axis=0)
        return x

    q = rows(q_ref) * (GLA_KEY_DIM ** -0.5)
    k = rows(k_ref)
    v_f32 = rows(v_ref)
    v = v_f32.astype(BF16)
    a = rows(a_ref)

    a1 = a.astype(BF16)
    r1 = a - a1.astype(F32)
    a2 = r1.astype(BF16)
    a3 = (r1 - a2.astype(F32)).astype(BF16)
    pre = pre_ref[...]
    sums = (jnp.dot(pre, a1, preferred_element_type=F32) + jnp.dot(pre, a2, preferred_element_type=F32)
            + jnp.dot(pre, a3, preferred_element_type=F32))

    cum = sums[0:c]
    state_t = st_ref[...]
    o = lax.dot_general((q * jnp.exp(cum)).astype(BF16), state_t.astype(BF16), NT_DIMS,
                        preferred_element_type=F32)

    scores = lax.dot_general(q.astype(BF16), k.astype(BF16), NT_DIMS, preferred_element_type=F32) * msk_ref[0]
    for lv in range(1, n_levels):
        e = jnp.exp(sums[lv * c:(lv + 1) * c])
        s_lv = lax.dot_general((q * e).astype(BF16), (k * e).astype(BF16), NT_DIMS, preferred_element_type=F32)
        scores = scores + s_lv * msk_ref[lv]
    o = o + jnp.dot(scores.astype(BF16), v, preferred_element_type=F32)

    last = cum[c - 1:c]
    k_dec = (k * jnp.exp(last - cum)).astype(BF16)
    upd_t = jnp.dot(v_f32.T.astype(BF16), k_dec, preferred_element_type=F32)
    new_state_t = jnp.exp(last) * state_t + upd_t
    st_ref[...] = new_state_t

    @pl.when(ci == pl.num_programs(2) - 1)
    def _():
        sout_ref[0, 0] = new_state_t.T

    o = o[0:n_rows]
    ms = jnp.mean(o * o, axis=-1, keepdims=True)
    og = og_ref[...]
    y = (o * lax.rsqrt(ms + NORM_EPS) * g_ref[...]) * (og * _sigmoid(og))
    y_ref[...] = y.astype(y_ref.dtype)


def _gla(u, log_a, gla_norm_g, state0, nseq, seqlen):
    m = u.shape[0]
    c = GLA_CHUNK
    n_rows = min(c, seqlen)
    assert seqlen % n_rows == 0 and n_rows % 8 == 0
    n_chunks = seqlen // n_rows
    pre_np, msk_np = _gla_tables(c)
    pre = jnp.asarray(pre_np, BF16)
    msk = jnp.asarray(msk_np, F32)
    gh, gk, gv = GLA_HEADS, GLA_KEY_DIM, GLA_VAL_DIM
    kv_ratio = gv // gk
    row = lambda b, hh, ci: b * n_chunks + ci
    in_specs = [
        pl.BlockSpec((n_rows, gk), lambda b, hh, ci: (row(b, hh, ci), hh)),
        pl.BlockSpec((n_rows, gk), lambda b, hh, ci: (row(b, hh, ci), gh + hh)),
        pl.BlockSpec((n_rows, gv), lambda b, hh, ci: (row(b, hh, ci), (2 * gh) // kv_ratio + hh)),
        pl.BlockSpec((n_rows, gv), lambda b, hh, ci: (row(b, hh, ci), (2 * gh) // kv_ratio + gh + hh)),
        pl.BlockSpec((n_rows, gk), lambda b, hh, ci: (row(b, hh, ci), hh)),
        pl.BlockSpec(pre.shape, lambda b, hh, ci: (0, 0)),
        pl.BlockSpec(msk.shape, lambda b, hh, ci: (0, 0, 0)),
        pl.BlockSpec((1, gv), lambda b, hh, ci: (0, 0)),
    ]
    args = [u, u, u, u, log_a, pre, msk, gla_norm_g.reshape(1, gv)]
    if state0 is not None:
        in_specs.append(pl.BlockSpec((1, 1, gk, gv), lambda b, hh, ci: (b, hh, 0, 0)))
        args.append(state0)
    kern = functools.partial(_gla_kernel, n_rows=n_rows, has_state0=state0 is not None)
    return pl.pallas_call(
        kern,
        grid=(nseq, gh, n_chunks),
        in_specs=in_specs,
        out_specs=[pl.BlockSpec((n_rows, gv), lambda b, hh, ci: (row(b, hh, ci), hh)),
                   pl.BlockSpec((1, 1, gk, gv), lambda b, hh, ci: (b, hh, 0, 0))],
        out_shape=[jax.ShapeDtypeStruct((m, gh * gv), BF16 if n_rows % 16 == 0 else F32),
                   jax.ShapeDtypeStruct((nseq, gh, gk, gv), F32)],
        scratch_shapes=[pltpu.VMEM((gv, gk), F32)],
        compiler_params=_params("parallel", "parallel", "arbitrary"),
        name="gla",
    )(*args)


def _tile(m, pref):
    t = min(m, pref)
    assert m % t == 0
    return t


def _run_group(x, attend, state0, wts):
    nseq, seqlen, d = x.shape
    m = nseq * seqlen
    x2 = x.reshape(m, d)
    tm = _tile(m, 512)

    h = _rmsnorm(x2, wts["norm1_g"], tm)
    tmh = _tile(seqlen, 1024) if seqlen >= 256 else m
    q = _proj_heads(h, wts["w_q"], wts["q_norm_g"], nseq, seqlen, True, tmh, 4, "proj_q")
    k = _proj_heads(h, wts["w_k"], wts["k_norm_g"], nseq, seqlen, True, tmh, 4, "proj_k")
    v = _proj_heads(h, wts["w_v"], wts["k_norm_g"], nseq, seqlen, False, tmh, 4, "proj_v")
    u = _matmul(h, wts["w_gla"], F32, _tile(m, 1024), 512, "proj_gla")
    log_a = _loga(h, wts["w_ag"], wts["w_a2"], wts["b_a"], tm)

    y_attn = attend(q, k, v)
    y_gla, gla_state = _gla(u, log_a, wts["gla_norm_g"], state0, nseq, seqlen)

    mixed = _merge(h, y_attn.reshape(m, -1), y_gla, wts["w_gate_a"], wts["w_gate_g"],
                   wts["w_branch_attn"], wts["w_branch_gla"], tm, 512)
    x1 = _matmul_residual(mixed, wts["w_out"], x2, _tile(m, 1024), 512, "out_proj")
    y = _mlp(x1, wts["norm2_g"], wts["w_up"], wts["w_down"], tm, 512)
    return y.reshape(nseq, seqlen, d), k, v, gla_state


def kernel(x_prompt, x_sample, cache_k, cache_v, state_gla, page_table, norm1_g, w_in, q_norm_g, k_norm_g,
           w_a2, b_a, gla_norm_g, w_branch_attn, w_branch_gla, w_out, norm2_g, w_up, w_down):
    depth = w_in.shape[0]
    aw = ATTN_HEADS * HEAD_DIM
    gkw = GLA_HEADS * GLA_KEY_DIM
    gvw = GLA_HEADS * GLA_VAL_DIM
    d_model = x_prompt.shape[-1]
    off = np.cumsum([0, aw, aw, aw, 2 * gkw + 2 * gvw, GLA_GATE_RANK, d_model, d_model])

    yp, ys = x_prompt, x_sample
    outs = [[] for _ in range(6)]
    for l in range(depth):
        w_l = w_in[l]
        seg = lambda i: w_l[:, off[i]:off[i + 1]].astype(BF16)
        wts = {
            "norm1_g": norm1_g[l], "q_norm_g": q_norm_g[l], "k_norm_g": k_norm_g[l],
            "w_q": seg(0), "w_k": seg(1), "w_v": seg(2), "w_gla": seg(3),
            "w_ag": jnp.pad(seg(4), ((0, 0), (0, LANES - GLA_GATE_RANK))),
            "w_gate_a": seg(5), "w_gate_g": seg(6),
            "w_a2": jnp.pad(w_a2[l], ((0, LANES - GLA_GATE_RANK), (0, 0))),
            "b_a": b_a[l].reshape(1, -1), "gla_norm_g": gla_norm_g[l],
            "w_branch_attn": w_branch_attn[l].astype(BF16), "w_branch_gla": w_branch_gla[l].astype(BF16),
            "w_out": w_out[l].astype(BF16), "norm2_g": norm2_g[l],
            "w_up": w_up[l].astype(BF16), "w_down": w_down[l].astype(BF16),
        }
        yp, k_l, v_l, s_l = _run_group(yp, _moba_prompt, None, wts)
        outs[0].append(k_l); outs[1].append(v_l); outs[2].append(s_l)
        attend_s = lambda q, k, v: _moba_sample(q, k, v, cache_k, cache_v, page_table, l)
        ys, k_l, v_l, s_l = _run_group(ys, attend_s, state_gla[l], wts)
        outs[3].append(k_l); outs[4].append(v_l); outs[5].append(s_l)
    stack = lambda xs: xs[0][None] if len(xs) == 1 else jnp.stack(xs)
    return (yp, ys) + tuple(stack(o) for o in outs)
```

```python
import functools

import numpy as np
import jax
import jax.numpy as jnp
from jax import lax
from jax.experimental import pallas as pl
from jax.experimental.pallas import tpu as pltpu

F32 = jnp.float32
BF16 = jnp.bfloat16
HIGHEST = lax.Precision.HIGHEST

NORM_EPS = 1e-6
ATTN_HEADS = 16
HEAD_DIM = 128
MOBA_BLOCK = 256
MOBA_TOPK = 3
GLA_HEADS = 4
GLA_KEY_DIM = 256
GLA_VAL_DIM = 512
GLA_GATE_RANK = 16
GLA_GATE_NORMALIZER = 16.0

LANES = 128
VMEM_LIMIT_BYTES = 56 * 1024 * 1024
GLA_CHUNK = 128
GLA_SHORT_CHUNK = 16
MASK_VALUE = -1e30

NT_DIMS = (((1,), (1,)), ((), ()))
TN_DIMS = (((0,), (0,)), ((), ()))


def _params(*sem):
    return pltpu.CompilerParams(dimension_semantics=sem, vmem_limit_bytes=VMEM_LIMIT_BYTES)


def _sigmoid(x):
    return 1.0 / (1.0 + jnp.exp(-x))


def _rmsnorm_kernel(x_ref, g_ref, o_ref):
    x = x_ref[...]
    ms = jnp.mean(x * x, axis=-1, keepdims=True)
    o_ref[...] = (x * lax.rsqrt(ms + NORM_EPS) * g_ref[...]).astype(o_ref.dtype)


def _rmsnorm(x, g, tm):
    m, d = x.shape
    return pl.pallas_call(
        _rmsnorm_kernel,
        grid=(m // tm,),
        in_specs=[pl.BlockSpec((tm, d), lambda i: (i, 0)), pl.BlockSpec((1, d), lambda i: (0, 0))],
        out_specs=pl.BlockSpec((tm, d), lambda i: (i, 0)),
        out_shape=jax.ShapeDtypeStruct((m, d), BF16),
        compiler_params=_params("parallel"),
        name="rmsnorm",
    )(x, g.reshape(1, d))


def _matmul_kernel(a_ref, w_ref, o_ref):
    o_ref[...] = jnp.dot(a_ref[...], w_ref[...], preferred_element_type=F32).astype(o_ref.dtype)


def _matmul(a, w, col0, n, out_dtype, tm, tn, name):
    m, k = a.shape
    assert col0 % tn == 0 and n % tn == 0
    return pl.pallas_call(
        _matmul_kernel,
        grid=(m // tm, n // tn),
        in_specs=[pl.BlockSpec((tm, k), lambda i, j: (i, 0)),
                  pl.BlockSpec((k, tn), lambda i, j: (0, col0 // tn + j))],
        out_specs=pl.BlockSpec((tm, tn), lambda i, j: (i, j)),
        out_shape=jax.ShapeDtypeStruct((m, n), out_dtype),
        compiler_params=_params("parallel", "parallel"),
        name=name,
    )(a, w)


def _matmul_residual_kernel(a_ref, w_ref, r_ref, o_ref):
    o_ref[...] = r_ref[...] + jnp.dot(a_ref[...], w_ref[...], preferred_element_type=F32)


def _matmul_residual(a, w, r, tm, tn, name):
    m, k = a.shape
    n = w.shape[1]
    return pl.pallas_call(
        _matmul_residual_kernel,
        grid=(m // tm, n // tn),
        in_specs=[pl.BlockSpec((tm, k), lambda i, j: (i, 0)), pl.BlockSpec((k, tn), lambda i, j: (0, j)),
                  pl.BlockSpec((tm, tn), lambda i, j: (i, j))],
        out_specs=pl.BlockSpec((tm, tn), lambda i, j: (i, j)),
        out_shape=jax.ShapeDtypeStruct((m, n), F32),
        compiler_params=_params("parallel", "parallel"),
        name=name,
    )(a, w, r)


def _proj_heads_kernel(h_ref, w_ref, g_ref, o_ref, *, normalize, heads_per_tile, seq_major):
    res = jnp.dot(h_ref[...], w_ref[...], preferred_element_type=F32)
    for i in range(heads_per_tile):
        r = res[:, i * HEAD_DIM:(i + 1) * HEAD_DIM]
        if normalize:
            ms = jnp.mean(r * r, axis=-1, keepdims=True)
            r = r * lax.rsqrt(ms + NORM_EPS) * g_ref[...]
        if seq_major:
            o_ref[0, i] = r
        else:
            o_ref[:, i] = r.reshape(o_ref.shape[0], o_ref.shape[2], HEAD_DIM)


def _proj_heads(h, w, col0, nh, g, nseq, seqlen, normalize, tm, heads_per_tile, name):
    m, d = h.shape
    tn = heads_per_tile * HEAD_DIM
    assert col0 % tn == 0 and nh % heads_per_tile == 0
    ct0 = col0 // tn
    out_shape = jax.ShapeDtypeStruct((nseq, nh, seqlen, HEAD_DIM), F32)
    gain = g.reshape(1, HEAD_DIM)
    if seqlen % tm == 0:
        spt = seqlen // tm
        kern = functools.partial(_proj_heads_kernel, normalize=normalize, heads_per_tile=heads_per_tile, seq_major=True)
        return pl.pallas_call(
            kern,
            grid=(nseq, spt, nh // heads_per_tile),
            in_specs=[pl.BlockSpec((tm, d), lambda b, s, j: (b * spt + s, 0)),
                      pl.BlockSpec((d, tn), lambda b, s, j: (0, ct0 + j)),
                      pl.BlockSpec((1, HEAD_DIM), lambda b, s, j: (0, 0))],
            out_specs=pl.BlockSpec((1, heads_per_tile, tm, HEAD_DIM), lambda b, s, j: (b, j, s, 0)),
            out_shape=out_shape,
            compiler_params=_params("parallel", "parallel", "parallel"),
            name=name,
        )(h, w, gain)
    assert tm == m
    kern = functools.partial(_proj_heads_kernel, normalize=normalize, heads_per_tile=heads_per_tile, seq_major=False)
    return pl.pallas_call(
        kern,
        grid=(nh // heads_per_tile,),
        in_specs=[pl.BlockSpec((m, d), lambda j: (0, 0)),
                  pl.BlockSpec((d, tn), lambda j: (0, ct0 + j)),
                  pl.BlockSpec((1, HEAD_DIM), lambda j: (0, 0))],
        out_specs=pl.BlockSpec((nseq, heads_per_tile, seqlen, HEAD_DIM), lambda j: (0, j, 0, 0)),
        out_shape=out_shape,
        compiler_params=_params("parallel"),
        name=name,
    )(h, w, gain)


def _loga_kernel(h_ref, wag_ref, wa2_ref, ba_ref, o_ref):
    ag = jnp.dot(h_ref[...], wag_ref[...], preferred_element_type=F32)
    z = jnp.dot(ag, wa2_ref[...], preferred_element_type=F32, precision=HIGHEST) + ba_ref[...]
    ls = jnp.minimum(z, 0.0) - jnp.log1p(jnp.exp(-jnp.abs(z)))
    o_ref[...] = ls * (1.0 / GLA_GATE_NORMALIZER)


def _loga(h, wag, wa2, ba, tm):
    m, d = h.shape
    n = wa2.shape[1]
    return pl.pallas_call(
        _loga_kernel,
        grid=(m // tm,),
        in_specs=[pl.BlockSpec((tm, d), lambda i: (i, 0)), pl.BlockSpec((d, LANES), lambda i: (0, 0)),
                  pl.BlockSpec((LANES, n), lambda i: (0, 0)), pl.BlockSpec((1, n), lambda i: (0, 0))],
        out_specs=pl.BlockSpec((tm, n), lambda i: (i, 0)),
        out_shape=jax.ShapeDtypeStruct((m, n), F32),
        compiler_params=_params("parallel"),
        name="gla_log_alpha",
    )(h, wag, wa2, ba)


def _merge_kernel(h_ref, ya_ref, yg_ref, wga_ref, wgg_ref, wba_ref, wbg_ref, o_ref):
    h = h_ref[...]
    ga = jnp.dot(h, wga_ref[...], preferred_element_type=F32)
    gg = jnp.dot(h, wgg_ref[...], preferred_element_type=F32)
    ba = jnp.dot(ya_ref[...].astype(BF16), wba_ref[...], preferred_element_type=F32)
    bg = jnp.dot(yg_ref[...].astype(BF16), wbg_ref[...], preferred_element_type=F32)
    o_ref[...] = (_sigmoid(ga) * ba + _sigmoid(gg) * bg).astype(o_ref.dtype)


def _merge(h, y_attn, y_gla, w_gates, wba, wbg, tm, tn):
    m, d = h.shape
    n = wba.shape[1]
    assert w_gates.shape[1] == 2 * n and n % tn == 0
    row = lambda i, j: (i, 0)
    col = lambda i, j: (0, j)
    return pl.pallas_call(
        _merge_kernel,
        grid=(m // tm, n // tn),
        in_specs=[pl.BlockSpec((tm, d), row), pl.BlockSpec((tm, y_attn.shape[1]), row),
                  pl.BlockSpec((tm, y_gla.shape[1]), row),
                  pl.BlockSpec((d, tn), col), pl.BlockSpec((d, tn), lambda i, j: (0, n // tn + j)),
                  pl.BlockSpec((wba.shape[0], tn), col), pl.BlockSpec((wbg.shape[0], tn), col)],
        out_specs=pl.BlockSpec((tm, tn), lambda i, j: (i, j)),
        out_shape=jax.ShapeDtypeStruct((m, n), BF16),
        compiler_params=_params("parallel", "parallel"),
        name="branch_merge",
    )(h, y_attn, y_gla, w_gates, w_gates, wba, wbg)


def _mlp_kernel(x_ref, g_ref, wup_ref, wdn_ref, o_ref, h_ref, acc_ref):
    f = pl.program_id(1)

    @pl.when(f == 0)
    def _():
        x = x_ref[...]
        ms = jnp.mean(x * x, axis=-1, keepdims=True)
        h_ref[...] = (x * lax.rsqrt(ms + NORM_EPS) * g_ref[...]).astype(BF16)
        acc_ref[...] = jnp.zeros_like(acc_ref)

    u = jnp.dot(h_ref[...], wup_ref[...], preferred_element_type=F32)
    a = jnp.square(jnp.maximum(u, 0.0)).astype(BF16)
    acc_ref[...] += jnp.dot(a, wdn_ref[...], preferred_element_type=F32)

    @pl.when(f == pl.num_programs(1) - 1)
    def _():
        o_ref[...] = x_ref[...] + acc_ref[...]


def _mlp(x, g, wup, wdn, tm, tf):
    m, d = x.shape
    ff = wup.shape[1]
    return pl.pallas_call(
        _mlp_kernel,
        grid=(m // tm, ff // tf),
        in_specs=[pl.BlockSpec((tm, d), lambda i, f: (i, 0)), pl.BlockSpec((1, d), lambda i, f: (0, 0)),
                  pl.BlockSpec((d, tf), lambda i, f: (0, f)), pl.BlockSpec((tf, d), lambda i, f: (f, 0))],
        out_specs=pl.BlockSpec((tm, d), lambda i, f: (i, 0)),
        out_shape=jax.ShapeDtypeStruct((m, d), F32),
        scratch_shapes=[pltpu.VMEM((tm, d), BF16), pltpu.VMEM((tm, d), F32)],
        compiler_params=_params("parallel", "arbitrary"),
        name="mlp",
    )(x, g.reshape(1, d), wup, wdn)


def _top_blocks(scores, valid, n_sel, axis):
    idxf = lax.broadcasted_iota(jnp.int32, scores.shape, axis).astype(F32)
    sv = jnp.where(valid, scores, -jnp.inf)
    sel = jnp.zeros(scores.shape, F32)
    for _ in range(n_sel):
        mx = jnp.max(sv, axis=axis, keepdims=True)
        idx = jnp.min(jnp.where(sv == mx, idxf, float(LANES)), axis=axis, keepdims=True)
        pick = idxf == idx
        sel = jnp.where(pick, 1.0, sel)
        sv = jnp.where(pick, -jnp.inf, sv)
    return jnp.where(valid, sel, 0.0)


def _moba_prompt_kernel(q_ref, k_ref, v_ref, onehot_ref, o_ref, *, n_blocks):
    scale = HEAD_DIM ** -0.5
    blk = MOBA_BLOCK
    q = q_ref[0, 0]
    k = k_ref[0, 0]
    seq = q.shape[0]
    kmean = jnp.concatenate(
        [jnp.sum(k[j * blk:(j + 1) * blk], axis=0, keepdims=True) / blk for j in range(n_blocks)], axis=0)
    gate_t = lax.dot_general(kmean, q, NT_DIMS, precision=HIGHEST, preferred_element_type=F32)
    key_blk = lax.broadcasted_iota(jnp.int32, gate_t.shape, 0)
    qry_blk = lax.broadcasted_iota(jnp.int32, gate_t.shape, 1) // blk
    past = key_blk < qry_blk
    sel_t = _top_blocks(gate_t, past, min(MOBA_TOPK, n_blocks - 1), axis=0)
    allowed = (sel_t > 0.5) | (key_blk == qry_blk)
    mask_t = jnp.where(allowed, 0.0, MASK_VALUE)
    mask_t = jnp.concatenate([mask_t, jnp.zeros((LANES - n_blocks, seq), F32)], axis=0).astype(BF16)

    row = lax.broadcasted_iota(jnp.int32, (blk, blk), 0)
    col = lax.broadcasted_iota(jnp.int32, (blk, blk), 1)
    eye = jnp.where(row == col, 1.0, 0.0).astype(BF16)
    k_aug = jnp.concatenate([k.astype(BF16), onehot_ref[...]], axis=1)
    v_b = v_ref[0, 0].astype(BF16)

    for i in range(n_blocks):
        lo, hi = i * blk, (i + 1) * blk
        mask_i = lax.dot_general(eye, mask_t[:, lo:hi], NT_DIMS, preferred_element_type=F32)
        q_aug = jnp.concatenate([q[lo:hi].astype(BF16), mask_i.astype(BF16)], axis=1)
        s = lax.dot_general(q_aug, k_aug[0:hi], NT_DIMS, preferred_element_type=F32) * scale
        s_own = jnp.where(col <= row, s[:, lo:hi], MASK_VALUE)
        s = s_own if i == 0 else jnp.concatenate([s[:, 0:lo], s_own], axis=1)
        m = jnp.max(s, axis=-1, keepdims=True)
        p = jnp.exp(s - m)
        l = jnp.sum(p, axis=-1, keepdims=True)
        o = jnp.dot(p.astype(BF16), v_b[0:hi], preferred_element_type=F32)
        o_ref[0, lo:hi, :] = (o / l).astype(o_ref.dtype)


def _moba_prompt(q, k, v):
    b, h, s, d = q.shape
    assert s % MOBA_BLOCK == 0 and d == HEAD_DIM
    nb = s // MOBA_BLOCK
    assert nb % 8 == 0 and nb <= LANES
    onehot = jnp.asarray(np.arange(s)[:, None] // MOBA_BLOCK == np.arange(LANES)[None, :], BF16)
    head_map = lambda bi, hi: (bi, hi, 0, 0)
    return pl.pallas_call(
        functools.partial(_moba_prompt_kernel, n_blocks=nb),
        grid=(b, h),
        in_specs=[pl.BlockSpec((1, 1, s, d), head_map), pl.BlockSpec((1, 1, s, d), head_map),
                  pl.BlockSpec((1, 1, s, d), head_map), pl.BlockSpec((s, LANES), lambda bi, hi: (0, 0))],
        out_specs=pl.BlockSpec((1, s, d), lambda bi, hi: (bi, 0, hi)),
        out_shape=jax.ShapeDtypeStruct((b, s, h * d), BF16),
        compiler_params=_params("parallel", "parallel"),
        name="moba_prompt",
    )(q, k, v, onehot)


PAGE_GROUP = 8


def _moba_sample_probs_kernel(pt_ref, q_ref, kn_ref, *rest, n_pages, pages_per_block, group):
    kc_refs = rest[:group]
    p_ref, pown_ref, inv_ref, bsum_ref = rest[group:]
    step = pl.program_id(1)
    n_heads = q_ref.shape[1]
    n_tok = q_ref.shape[2]
    n_blocks = n_pages // pages_per_block
    scale = HEAD_DIM ** -0.5

    @pl.when(step == 0)
    def _():
        bsum_ref[...] = jnp.zeros_like(bsum_ref)

    page_sums = []
    for g in range(group):
        logits, sums = [], []
        for h in range(n_heads):
            kh = kc_refs[g][0, 0, h]
            sums.append(jnp.sum(kh, axis=0, keepdims=True))
            logits.append(lax.dot_general(q_ref[0, h].astype(BF16), kh.astype(BF16), NT_DIMS,
                                          preferred_element_type=F32) * scale)
        p_ref[0, step * group + g] = jnp.stack(logits)
        page_sums.append(sums)
    for bb in range(group // pages_per_block):
        blk = step * (group // pages_per_block) + bb
        for h in range(n_heads):
            tot = page_sums[bb * pages_per_block][h]
            for pp in range(1, pages_per_block):
                tot = tot + page_sums[bb * pages_per_block + pp][h]
            bsum_ref[h, pl.ds(blk, 1), :] = tot

    @pl.when(step == pl.num_programs(1) - 1)
    def _():
        sels, owns = [], []
        for h in range(n_heads):
            qh = q_ref[0, h]
            kmean = bsum_ref[h] / MOBA_BLOCK
            gate = lax.dot_general(qh, kmean, NT_DIMS, precision=HIGHEST, preferred_element_type=F32)
            lane = lax.broadcasted_iota(jnp.int32, gate.shape, 1)
            sels.append(_top_blocks(gate, lane < n_blocks, min(MOBA_TOPK, n_blocks), axis=1))
            kn = jnp.concatenate([kn_ref[0, h], jnp.zeros((LANES - n_tok, HEAD_DIM), F32)], axis=0).astype(BF16)
            own = lax.dot_general(qh.astype(BF16), kn, NT_DIMS, preferred_element_type=F32) * scale
            row = lax.broadcasted_iota(jnp.int32, own.shape, 0)
            owns.append(jnp.where(lane <= row, own, MASK_VALUE))
        sel = jnp.stack(sels)
        own = jnp.stack(owns)
        picked = [sel[:, :, b:b + 1] > 0.5 for b in range(n_blocks)]
        mrun = own
        for page in range(n_pages):
            mrun = jnp.maximum(mrun, jnp.where(picked[page // pages_per_block], p_ref[0, page], MASK_VALUE))
        m = jnp.max(mrun, axis=-1, keepdims=True)
        e_own = jnp.exp(own - m)
        lrun = e_own
        for page in range(n_pages):
            e = jnp.where(picked[page // pages_per_block], jnp.exp(p_ref[0, page] - m), 0.0)
            p_ref[0, page] = e
            lrun = lrun + e
        pown_ref[0] = e_own
        inv_ref[0] = jnp.broadcast_to(1.0 / jnp.sum(lrun, axis=-1, keepdims=True), inv_ref.shape[1:])


def _moba_sample_values_kernel(pt_ref, p_ref, pown_ref, inv_ref, vn_ref, *rest, group):
    vc_refs = rest[:group]
    o_ref = rest[group]
    step = pl.program_id(1)
    n_heads = vn_ref.shape[1]
    n_tok = vn_ref.shape[2]
    head = lambda h: slice(h * HEAD_DIM, (h + 1) * HEAD_DIM)

    @pl.when(step == 0)
    def _():
        for h in range(n_heads):
            pown = pown_ref[0, h]
            vn = vn_ref[0, h]
            own = pown[:, 0:1] * vn[0:1, :]
            for t in range(1, n_tok):
                own = own + pown[:, t:t + 1] * vn[t:t + 1, :]
            o_ref[0, :, head(h)] = own

    for h in range(n_heads):
        w = jnp.concatenate([p_ref[0, g, h] for g in range(group)], axis=1).astype(BF16)
        vals = jnp.concatenate([vc_refs[g][0, 0, h] for g in range(group)], axis=0).astype(BF16)
        o_ref[0, :, head(h)] += jnp.dot(w, vals, preferred_element_type=F32)

    @pl.when(step == pl.num_programs(1) - 1)
    def _():
        for h in range(n_heads):
            o_ref[0, :, head(h)] = o_ref[0, :, head(h)] * inv_ref[0, h]


def _moba_sample(q, k_new, v_new, cache_k, cache_v, page_table, layer):
    db, h, t, d = q.shape
    n_pages = page_table.shape[1]
    page = cache_k.shape[3]
    ppb = MOBA_BLOCK // page
    group = PAGE_GROUP
    assert page == LANES and d == HEAD_DIM and MOBA_BLOCK % page == 0
    assert (n_pages * page) % MOBA_BLOCK == 0 and t <= MOBA_BLOCK
    assert n_pages // ppb <= LANES and n_pages // ppb >= MOBA_TOPK
    assert n_pages % group == 0 and group % ppb == 0
    n_steps = n_pages // group
    pt = page_table.reshape(-1).astype(jnp.int32)

    seq4 = lambda di, si, pt_ref: (di, 0, 0, 0)
    cache_specs = [
        pl.BlockSpec((1, 1, h, page, d),
                     lambda di, si, pt_ref, g=g: (layer, pt_ref[di * n_pages + si * group + g], 0, 0, 0))
        for g in range(group)]

    probs, p_own, inv = pl.pallas_call(
        functools.partial(_moba_sample_probs_kernel, n_pages=n_pages, pages_per_block=ppb, group=group),
        grid_spec=pltpu.PrefetchScalarGridSpec(
            num_scalar_prefetch=1,
            grid=(db, n_steps),
            in_specs=[pl.BlockSpec((1, h, t, d), seq4), pl.BlockSpec((1, h, t, d), seq4)] + cache_specs,
            out_specs=[pl.BlockSpec((1, n_pages, h, t, page), lambda di, si, pt_ref: (di, 0, 0, 0, 0)),
                       pl.BlockSpec((1, h, t, LANES), seq4), pl.BlockSpec((1, h, t, LANES), seq4)],
            scratch_shapes=[pltpu.VMEM((h, LANES, d), F32)],
        ),
        out_shape=[jax.ShapeDtypeStruct((db, n_pages, h, t, page), F32),
                   jax.ShapeDtypeStruct((db, h, t, LANES), F32),
                   jax.ShapeDtypeStruct((db, h, t, LANES), F32)],
        compiler_params=_params("parallel", "arbitrary"),
        name="moba_sample_probs",
    )(pt, q, k_new, *([cache_k] * group))

    return pl.pallas_call(
        functools.partial(_moba_sample_values_kernel, group=group),
        grid_spec=pltpu.PrefetchScalarGridSpec(
            num_scalar_prefetch=1,
            grid=(db, n_steps),
            in_specs=[pl.BlockSpec((1, group, h, t, page), lambda di, si, pt_ref: (di, si, 0, 0, 0)),
                      pl.BlockSpec((1, h, t, LANES), seq4), pl.BlockSpec((1, h, t, LANES), seq4),
                      pl.BlockSpec((1, h, t, d), seq4)] + cache_specs,
            out_specs=pl.BlockSpec((1, t, h * d), lambda di, si, pt_ref: (di, 0, 0)),
        ),
        out_shape=jax.ShapeDtypeStruct((db, t, h * d), F32),
        compiler_params=_params("parallel", "arbitrary"),
        name="moba_sample_values",
    )(pt, probs, p_own, inv, v_new, *([cache_v] * group))


def _gla_tables(c, key_rows):
    widths = []
    w = c
    while w >= 2:
        widths.append(w)
        w //= 2
    t = np.arange(c)
    prefix = [np.tril(np.ones((c, c), np.float32))]
    masks = [np.eye(c, dtype=np.float32)]
    for w in widths:
        base = (t // w) * w
        mid = base + w // 2
        upper = t >= mid
        pm = np.zeros((c, c), np.float32)
        for r in range(c):
            if upper[r]:
                pm[r, mid[r]:r + 1] = 1.0
            else:
                pm[r, r + 1:mid[r]] = 1.0
        prefix.append(pm)
        same = base[:, None] == base[None, :]
        masks.append((same & upper[:, None] & (~upper)[None, :]).astype(np.float32))
    pad = ((0, 0), (0, key_rows - c))
    return np.pad(np.concatenate(prefix, axis=0), pad), np.pad(np.stack(masks, axis=0), ((0, 0),) + pad)


def _pad_rows(x, n):
    if x.shape[0] == n:
        return x
    return jnp.concatenate([x, jnp.zeros((n - x.shape[0], x.shape[1]), x.dtype)], axis=0)


def _gla_kernel(*refs, has_state0):
    if has_state0:
        q_ref, k_ref, v_ref, og_ref, a_ref, pre_ref, msk_ref, g_ref, s0_ref, y_ref, sout_ref, st_ref = refs
    else:
        q_ref, k_ref, v_ref, og_ref, a_ref, pre_ref, msk_ref, g_ref, y_ref, sout_ref, st_ref = refs
        s0_ref = None
    ci = pl.program_id(2)
    n_levels, c, kr = msk_ref.shape
    n_rows = q_ref.shape[0]

    @pl.when(ci == 0)
    def _():
        if has_state0:
            st_ref[...] = s0_ref[0, 0].T
        else:
            st_ref[...] = jnp.zeros_like(st_ref)

    q = _pad_rows(q_ref[...], c) * (GLA_KEY_DIM ** -0.5)
    k = _pad_rows(k_ref[...], c)
    v_f32 = _pad_rows(v_ref[...], kr)
    v = v_f32.astype(BF16)
    a = _pad_rows(a_ref[...], kr)

    a1 = a.astype(BF16)
    r1 = a - a1.astype(F32)
    a2 = r1.astype(BF16)
    a3 = (r1 - a2.astype(F32)).astype(BF16)
    pre = pre_ref[...]
    sums = (jnp.dot(pre, a1, preferred_element_type=F32) + jnp.dot(pre, a2, preferred_element_type=F32)
            + jnp.dot(pre, a3, preferred_element_type=F32))

    cum = sums[0:c]
    state_t = st_ref[...]
    o = lax.dot_general((q * jnp.exp(cum)).astype(BF16), state_t.astype(BF16), NT_DIMS,
                        preferred_element_type=F32)

    scores = lax.dot_general(q.astype(BF16), _pad_rows(k, kr).astype(BF16), NT_DIMS,
                             preferred_element_type=F32) * msk_ref[0]
    for lv in range(1, n_levels):
        e = jnp.exp(sums[lv * c:(lv + 1) * c])
        s_lv = lax.dot_general((q * e).astype(BF16), _pad_rows(k * e, kr).astype(BF16), NT_DIMS,
                               preferred_element_type=F32)
        scores = scores + s_lv * msk_ref[lv]
    o = o + jnp.dot(scores.astype(BF16), v, preferred_element_type=F32)

    last = cum[c - 1:c]
    k_dec = _pad_rows(k * jnp.exp(last - cum), kr).astype(BF16)
    upd_t = jnp.dot(v_f32.T.astype(BF16), k_dec, preferred_element_type=F32)
    new_state_t = jnp.exp(last) * state_t + upd_t
    st_ref[...] = new_state_t

    @pl.when(ci == pl.num_programs(2) - 1)
    def _():
        sout_ref[0, 0] = new_state_t.T

    o = o[0:n_rows]
    ms = jnp.mean(o * o, axis=-1, keepdims=True)
    og = og_ref[...]
    y = (o * lax.rsqrt(ms + NORM_EPS) * g_ref[...]) * (og * _sigmoid(og))
    y_ref[...] = y.astype(y_ref.dtype)


def _gla(u, log_a, gla_norm_g, state0, nseq, seqlen):
    m = u.shape[0]
    n_rows = min(GLA_CHUNK, seqlen)
    assert seqlen % n_rows == 0 and n_rows % 8 == 0
    n_chunks = seqlen // n_rows
    c = max(GLA_SHORT_CHUNK, n_rows)
    assert c & (c - 1) == 0 and c <= GLA_CHUNK
    pre_np, msk_np = _gla_tables(c, GLA_CHUNK)
    pre = jnp.asarray(pre_np, BF16)
    msk = jnp.asarray(msk_np, F32)
    gh, gk, gv = GLA_HEADS, GLA_KEY_DIM, GLA_VAL_DIM
    kv_ratio = gv // gk
    row = lambda b, hh, ci: b * n_chunks + ci
    in_specs = [
        pl.BlockSpec((n_rows, gk), lambda b, hh, ci: (row(b, hh, ci), hh)),
        pl.BlockSpec((n_rows, gk), lambda b, hh, ci: (row(b, hh, ci), gh + hh)),
        pl.BlockSpec((n_rows, gv), lambda b, hh, ci: (row(b, hh, ci), (2 * gh) // kv_ratio + hh)),
        pl.BlockSpec((n_rows, gv), lambda b, hh, ci: (row(b, hh, ci), (2 * gh) // kv_ratio + gh + hh)),
        pl.BlockSpec((n_rows, gk), lambda b, hh, ci: (row(b, hh, ci), hh)),
        pl.BlockSpec(pre.shape, lambda b, hh, ci: (0, 0)),
        pl.BlockSpec(msk.shape, lambda b, hh, ci: (0, 0, 0)),
        pl.BlockSpec((1, gv), lambda b, hh, ci: (0, 0)),
    ]
    args = [u, u, u, u, log_a, pre, msk, gla_norm_g.reshape(1, gv)]
    if state0 is not None:
        in_specs.append(pl.BlockSpec((1, 1, gk, gv), lambda b, hh, ci: (b, hh, 0, 0)))
        args.append(state0)
    kern = functools.partial(_gla_kernel, has_state0=state0 is not None)
    return pl.pallas_call(
        kern,
        grid=(nseq, gh, n_chunks),
        in_specs=in_specs,
        out_specs=[pl.BlockSpec((n_rows, gv), lambda b, hh, ci: (row(b, hh, ci), hh)),
                   pl.BlockSpec((1, 1, gk, gv), lambda b, hh, ci: (b, hh, 0, 0))],
        out_shape=[jax.ShapeDtypeStruct((m, gh * gv), BF16 if n_rows % 16 == 0 else F32),
                   jax.ShapeDtypeStruct((nseq, gh, gk, gv), F32)],
        scratch_shapes=[pltpu.VMEM((gv, gk), F32)],
        compiler_params=_params("parallel", "parallel", "arbitrary"),
        name="gla",
    )(*args)


def _tile(m, pref):
    t = min(m, pref)
    assert m % t == 0
    return t


def _run_group(x, attend, state0, wts):
    nseq, seqlen, d = x.shape
    m = nseq * seqlen
    x2 = x.reshape(m, d)
    tm = _tile(m, 512)

    h = _rmsnorm(x2, wts["norm1_g"], tm)
    tmh = _tile(seqlen, 1024) if seqlen >= 256 else m
    w_in, aw, nh = wts["w_in"], ATTN_HEADS * HEAD_DIM, ATTN_HEADS
    q = _proj_heads(h, w_in, 0, nh, wts["q_norm_g"], nseq, seqlen, True, tmh, 4, "proj_q")
    k = _proj_heads(h, w_in, aw, nh, wts["k_norm_g"], nseq, seqlen, True, tmh, 4, "proj_k")
    v = _proj_heads(h, w_in, 2 * aw, nh, wts["k_norm_g"], nseq, seqlen, False, tmh, 4, "proj_v")
    gla_cols = 2 * GLA_HEADS * (GLA_KEY_DIM + GLA_VAL_DIM)
    u = _matmul(h, w_in, 3 * aw, gla_cols, F32, _tile(m, 1024), 512, "proj_gla")
    log_a = _loga(h, wts["w_ag"], wts["w_a2"], wts["b_a"], tm)

    y_attn = attend(q, k, v)
    y_gla, gla_state = _gla(u, log_a, wts["gla_norm_g"], state0, nseq, seqlen)

    mixed = _merge(h, y_attn.reshape(m, -1), y_gla, wts["w_gates"],
                   wts["w_branch_attn"], wts["w_branch_gla"], tm, 512)
    x1 = _matmul_residual(mixed, wts["w_out"], x2, _tile(m, 1024), 512, "out_proj")
    y = _mlp(x1, wts["norm2_g"], wts["w_up"], wts["w_down"], tm, 512)
    return y.reshape(nseq, seqlen, d), k, v, gla_state


def kernel(x_prompt, x_sample, cache_k, cache_v, state_gla, page_table, norm1_g, w_in, q_norm_g, k_norm_g,
           w_a2, b_a, gla_norm_g, w_branch_attn, w_branch_gla, w_out, norm2_g, w_up, w_down):
    depth = w_in.shape[0]
    d_model = x_prompt.shape[-1]
    ag0 = 3 * ATTN_HEADS * HEAD_DIM + 2 * GLA_HEADS * (GLA_KEY_DIM + GLA_VAL_DIM)

    yp, ys = x_prompt, x_sample
    outs = [[] for _ in range(6)]
    for l in range(depth):
        w_l = w_in[l].astype(BF16)
        assert w_l.shape[1] == ag0 + GLA_GATE_RANK + 2 * d_model
        wts = {
            "norm1_g": norm1_g[l], "q_norm_g": q_norm_g[l], "k_norm_g": k_norm_g[l],
            "w_in": w_l,
            "w_ag": jnp.pad(w_l[:, ag0:ag0 + GLA_GATE_RANK], ((0, 0), (0, LANES - GLA_GATE_RANK))),
            "w_gates": w_l[:, ag0 + GLA_GATE_RANK:],
            "w_a2": jnp.pad(w_a2[l], ((0, LANES - GLA_GATE_RANK), (0, 0))),
            "b_a": b_a[l].reshape(1, -1), "gla_norm_g": gla_norm_g[l],
            "w_branch_attn": w_branch_attn[l].astype(BF16), "w_branch_gla": w_branch_gla[l].astype(BF16),
            "w_out": w_out[l].astype(BF16), "norm2_g": norm2_g[l],
            "w_up": w_up[l].astype(BF16), "w_down": w_down[l].astype(BF16),
        }
        yp, k_l, v_l, s_l = _run_group(yp, _moba_prompt, None, wts)
        outs[0].append(k_l); outs[1].append(v_l); outs[2].append(s_l)
        attend_s = lambda q, k, v: _moba_sample(q, k, v, cache_k, cache_v, page_table, l)
        ys, k_l, v_l, s_l = _run_group(ys, attend_s, state_gla[l], wts)
        outs[3].append(k_l); outs[4].append(v_l); outs[5].append(s_l)
    stack = lambda xs: xs[0][None] if len(xs) == 1 else jnp.stack(xs)
    return (yp, ys) + tuple(stack(o) for o in outs)
```

```python
import functools

import numpy as np
import jax
import jax.numpy as jnp
from jax import lax
from jax.experimental import pallas as pl
from jax.experimental.pallas import tpu as pltpu

F32 = jnp.float32
BF16 = jnp.bfloat16
HIGHEST = lax.Precision.HIGHEST

NORM_EPS = 1e-6
ATTN_HEADS = 16
HEAD_DIM = 128
MOBA_BLOCK = 256
MOBA_TOPK = 3
GLA_HEADS = 4
GLA_KEY_DIM = 256
GLA_VAL_DIM = 512
GLA_GATE_RANK = 16
GLA_GATE_NORMALIZER = 16.0

LANES = 128
VMEM_LIMIT_BYTES = 56 * 1024 * 1024
GLA_CHUNK = 128
GLA_SHORT_CHUNK = 16
MASK_VALUE = -1e30
LOG2_E = 1.4426950408889634

NT_DIMS = (((1,), (1,)), ((), ()))
TN_DIMS = (((0,), (0,)), ((), ()))


def _params(*sem):
    return pltpu.CompilerParams(dimension_semantics=sem, vmem_limit_bytes=VMEM_LIMIT_BYTES)


def _sigmoid(x):
    return 1.0 / (1.0 + jnp.exp(-x))


def _rmsnorm_kernel(x_ref, g_ref, o_ref):
    x = x_ref[...]
    ms = jnp.mean(x * x, axis=-1, keepdims=True)
    o_ref[...] = (x * lax.rsqrt(ms + NORM_EPS) * g_ref[...]).astype(o_ref.dtype)


def _rmsnorm(x, g, tm):
    m, d = x.shape
    return pl.pallas_call(
        _rmsnorm_kernel,
        grid=(m // tm,),
        in_specs=[pl.BlockSpec((tm, d), lambda i: (i, 0)), pl.BlockSpec((1, d), lambda i: (0, 0))],
        out_specs=pl.BlockSpec((tm, d), lambda i: (i, 0)),
        out_shape=jax.ShapeDtypeStruct((m, d), BF16),
        compiler_params=_params("parallel"),
        name="rmsnorm",
    )(x, g.reshape(1, d))


def _matmul_kernel(a_ref, w_ref, o_ref):
    o_ref[...] = jnp.dot(a_ref[...], w_ref[...], preferred_element_type=F32).astype(o_ref.dtype)


def _matmul(a, w, col0, n, out_dtype, tm, tn, name):
    m, k = a.shape
    assert col0 % tn == 0 and n % tn == 0
    return pl.pallas_call(
        _matmul_kernel,
        grid=(m // tm, n // tn),
        in_specs=[pl.BlockSpec((tm, k), lambda i, j: (i, 0)),
                  pl.BlockSpec((k, tn), lambda i, j: (0, col0 // tn + j))],
        out_specs=pl.BlockSpec((tm, tn), lambda i, j: (i, j)),
        out_shape=jax.ShapeDtypeStruct((m, n), out_dtype),
        compiler_params=_params("parallel", "parallel"),
        name=name,
    )(a, w)


def _matmul_residual_kernel(a_ref, w_ref, r_ref, o_ref):
    o_ref[...] = r_ref[...] + jnp.dot(a_ref[...], w_ref[...], preferred_element_type=F32)


def _matmul_residual(a, w, r, tm, tn, name):
    m, k = a.shape
    n = w.shape[1]
    return pl.pallas_call(
        _matmul_residual_kernel,
        grid=(m // tm, n // tn),
        in_specs=[pl.BlockSpec((tm, k), lambda i, j: (i, 0)), pl.BlockSpec((k, tn), lambda i, j: (0, j)),
                  pl.BlockSpec((tm, tn), lambda i, j: (i, j))],
        out_specs=pl.BlockSpec((tm, tn), lambda i, j: (i, j)),
        out_shape=jax.ShapeDtypeStruct((m, n), F32),
        compiler_params=_params("parallel", "parallel"),
        name=name,
    )(a, w, r)


def _proj_heads_kernel(h_ref, w_ref, g_ref, o_ref, *, normalize, heads_per_tile, seq_major):
    res = jnp.dot(h_ref[...], w_ref[...], preferred_element_type=F32)
    for i in range(heads_per_tile):
        r = res[:, i * HEAD_DIM:(i + 1) * HEAD_DIM]
        if normalize:
            ms = jnp.mean(r * r, axis=-1, keepdims=True)
            r = r * lax.rsqrt(ms + NORM_EPS) * g_ref[...]
        if seq_major:
            o_ref[0, i] = r
        else:
            o_ref[:, i] = r.reshape(o_ref.shape[0], o_ref.shape[2], HEAD_DIM)


def _proj_heads(h, w, col0, nh, g, nseq, seqlen, normalize, tm, heads_per_tile, name):
    m, d = h.shape
    tn = heads_per_tile * HEAD_DIM
    assert col0 % tn == 0 and nh % heads_per_tile == 0
    ct0 = col0 // tn
    out_shape = jax.ShapeDtypeStruct((nseq, nh, seqlen, HEAD_DIM), F32)
    gain = g.reshape(1, HEAD_DIM)
    if seqlen % tm == 0:
        spt = seqlen // tm
        kern = functools.partial(_proj_heads_kernel, normalize=normalize, heads_per_tile=heads_per_tile, seq_major=True)
        return pl.pallas_call(
            kern,
            grid=(nseq, spt, nh // heads_per_tile),
            in_specs=[pl.BlockSpec((tm, d), lambda b, s, j: (b * spt + s, 0)),
                      pl.BlockSpec((d, tn), lambda b, s, j: (0, ct0 + j)),
                      pl.BlockSpec((1, HEAD_DIM), lambda b, s, j: (0, 0))],
            out_specs=pl.BlockSpec((1, heads_per_tile, tm, HEAD_DIM), lambda b, s, j: (b, j, s, 0)),
            out_shape=out_shape,
            compiler_params=_params("parallel", "parallel", "parallel"),
            name=name,
        )(h, w, gain)
    assert tm == m
    kern = functools.partial(_proj_heads_kernel, normalize=normalize, heads_per_tile=heads_per_tile, seq_major=False)
    return pl.pallas_call(
        kern,
        grid=(nh // heads_per_tile,),
        in_specs=[pl.BlockSpec((m, d), lambda j: (0, 0)),
                  pl.BlockSpec((d, tn), lambda j: (0, ct0 + j)),
                  pl.BlockSpec((1, HEAD_DIM), lambda j: (0, 0))],
        out_specs=pl.BlockSpec((nseq, heads_per_tile, seqlen, HEAD_DIM), lambda j: (0, j, 0, 0)),
        out_shape=out_shape,
        compiler_params=_params("parallel"),
        name=name,
    )(h, w, gain)


def _loga_kernel(h_ref, wag_ref, wa2_ref, ba_ref, o_ref):
    ag = jnp.dot(h_ref[...], wag_ref[...], preferred_element_type=F32)
    z = jnp.dot(ag, wa2_ref[...], preferred_element_type=F32, precision=HIGHEST) + ba_ref[...]
    ls = jnp.minimum(z, 0.0) - jnp.log1p(jnp.exp(-jnp.abs(z)))
    o_ref[...] = ls * (1.0 / GLA_GATE_NORMALIZER)


def _loga(h, wag, wa2, ba, tm):
    m, d = h.shape
    n = wa2.shape[1]
    return pl.pallas_call(
        _loga_kernel,
        grid=(m // tm,),
        in_specs=[pl.BlockSpec((tm, d), lambda i: (i, 0)), pl.BlockSpec((d, LANES), lambda i: (0, 0)),
                  pl.BlockSpec((LANES, n), lambda i: (0, 0)), pl.BlockSpec((1, n), lambda i: (0, 0))],
        out_specs=pl.BlockSpec((tm, n), lambda i: (i, 0)),
        out_shape=jax.ShapeDtypeStruct((m, n), F32),
        compiler_params=_params("parallel"),
        name="gla_log_alpha",
    )(h, wag, wa2, ba)


def _merge_kernel(h_ref, ya_ref, yg_ref, wga_ref, wgg_ref, wba_ref, wbg_ref, o_ref):
    h = h_ref[...]
    ga = jnp.dot(h, wga_ref[...], preferred_element_type=F32)
    gg = jnp.dot(h, wgg_ref[...], preferred_element_type=F32)
    ba = jnp.dot(ya_ref[...].astype(BF16), wba_ref[...], preferred_element_type=F32)
    bg = jnp.dot(yg_ref[...].astype(BF16), wbg_ref[...], preferred_element_type=F32)
    o_ref[...] = (_sigmoid(ga) * ba + _sigmoid(gg) * bg).astype(o_ref.dtype)


def _merge(h, y_attn, y_gla, w_gates, wba, wbg, tm, tn):
    m, d = h.shape
    n = wba.shape[1]
    assert w_gates.shape[1] == 2 * n and n % tn == 0
    row = lambda i, j: (i, 0)
    col = lambda i, j: (0, j)
    return pl.pallas_call(
        _merge_kernel,
        grid=(m // tm, n // tn),
        in_specs=[pl.BlockSpec((tm, d), row), pl.BlockSpec((tm, y_attn.shape[1]), row),
                  pl.BlockSpec((tm, y_gla.shape[1]), row),
                  pl.BlockSpec((d, tn), col), pl.BlockSpec((d, tn), lambda i, j: (0, n // tn + j)),
                  pl.BlockSpec((wba.shape[0], tn), col), pl.BlockSpec((wbg.shape[0], tn), col)],
        out_specs=pl.BlockSpec((tm, tn), lambda i, j: (i, j)),
        out_shape=jax.ShapeDtypeStruct((m, n), BF16),
        compiler_params=_params("parallel", "parallel"),
        name="branch_merge",
    )(h, y_attn, y_gla, w_gates, w_gates, wba, wbg)


def _mlp_kernel(x_ref, g_ref, wup_ref, wdn_ref, o_ref, h_ref, acc_ref):
    f = pl.program_id(1)

    @pl.when(f == 0)
    def _():
        x = x_ref[...]
        ms = jnp.mean(x * x, axis=-1, keepdims=True)
        h_ref[...] = (x * lax.rsqrt(ms + NORM_EPS) * g_ref[...]).astype(BF16)
        acc_ref[...] = jnp.zeros_like(acc_ref)

    u = jnp.dot(h_ref[...], wup_ref[...], preferred_element_type=F32)
    a = jnp.square(jnp.maximum(u, 0.0)).astype(BF16)
    acc_ref[...] += jnp.dot(a, wdn_ref[...], preferred_element_type=F32)

    @pl.when(f == pl.num_programs(1) - 1)
    def _():
        o_ref[...] = x_ref[...] + acc_ref[...]


def _mlp(x, g, wup, wdn, tm, tf):
    m, d = x.shape
    ff = wup.shape[1]
    return pl.pallas_call(
        _mlp_kernel,
        grid=(m // tm, ff // tf),
        in_specs=[pl.BlockSpec((tm, d), lambda i, f: (i, 0)), pl.BlockSpec((1, d), lambda i, f: (0, 0)),
                  pl.BlockSpec((d, tf), lambda i, f: (0, f)), pl.BlockSpec((tf, d), lambda i, f: (f, 0))],
        out_specs=pl.BlockSpec((tm, d), lambda i, f: (i, 0)),
        out_shape=jax.ShapeDtypeStruct((m, d), F32),
        scratch_shapes=[pltpu.VMEM((tm, d), BF16), pltpu.VMEM((tm, d), F32)],
        compiler_params=_params("parallel", "arbitrary"),
        name="mlp",
    )(x, g.reshape(1, d), wup, wdn)


def _top_blocks(scores, valid, n_sel, axis):
    idxf = lax.broadcasted_iota(jnp.int32, scores.shape, axis).astype(F32)
    sv = jnp.where(valid, scores, -jnp.inf)
    sel = jnp.zeros(scores.shape, F32)
    for _ in range(n_sel):
        mx = jnp.max(sv, axis=axis, keepdims=True)
        idx = jnp.min(jnp.where(sv == mx, idxf, float(LANES)), axis=axis, keepdims=True)
        pick = idxf == idx
        sel = jnp.where(pick, 1.0, sel)
        sv = jnp.where(pick, -jnp.inf, sv)
    return jnp.where(valid, sel, 0.0)


def _moba_prompt_kernel(q_ref, k_ref, v_ref, onehot_ref, o_ref, *, n_blocks):
    scale = HEAD_DIM ** -0.5
    blk = MOBA_BLOCK
    q = q_ref[0, 0]
    k = k_ref[0, 0]
    seq = q.shape[0]
    kmean = jnp.concatenate(
        [jnp.sum(k[j * blk:(j + 1) * blk], axis=0, keepdims=True) / blk for j in range(n_blocks)], axis=0)
    gate_t = lax.dot_general(kmean, q, NT_DIMS, precision=HIGHEST, preferred_element_type=F32)
    key_blk = lax.broadcasted_iota(jnp.int32, gate_t.shape, 0)
    qry_blk = lax.broadcasted_iota(jnp.int32, gate_t.shape, 1) // blk
    past = key_blk < qry_blk
    sel_t = _top_blocks(gate_t, past, min(MOBA_TOPK, n_blocks - 1), axis=0)
    allowed = (sel_t > 0.5) | (key_blk == qry_blk)
    mask_t = jnp.where(allowed, 0.0, MASK_VALUE)
    mask_t = jnp.concatenate([mask_t, jnp.zeros((LANES - n_blocks, seq), F32)], axis=0).astype(BF16)

    row = lax.broadcasted_iota(jnp.int32, (blk, blk), 0)
    col = lax.broadcasted_iota(jnp.int32, (blk, blk), 1)
    eye = jnp.where(row == col, 1.0, 0.0).astype(BF16)
    k_aug = jnp.concatenate([k.astype(BF16), onehot_ref[...]], axis=1)
    v_b = v_ref[0, 0].astype(BF16)

    for i in range(n_blocks):
        lo, hi = i * blk, (i + 1) * blk
        mask_i = lax.dot_general(eye, mask_t[:, lo:hi], NT_DIMS, preferred_element_type=F32)
        q_aug = jnp.concatenate([q[lo:hi].astype(BF16), mask_i.astype(BF16)], axis=1)
        s = lax.dot_general(q_aug, k_aug[0:hi], NT_DIMS, preferred_element_type=F32) * (scale * LOG2_E)
        s_own = jnp.where(col <= row, s[:, lo:hi], MASK_VALUE)
        m = jnp.max(s_own, axis=-1, keepdims=True)
        if i > 0:
            s_past = s[:, 0:lo]
            m = jnp.maximum(m, jnp.max(s_past, axis=-1, keepdims=True))
            p_past = jnp.exp2(s_past - m)
        p_own = jnp.exp2(s_own - m)
        l = jnp.sum(p_own, axis=-1, keepdims=True)
        o = jnp.dot(p_own.astype(BF16), v_b[lo:hi], preferred_element_type=F32)
        if i > 0:
            l = l + jnp.sum(p_past, axis=-1, keepdims=True)
            o = o + jnp.dot(p_past.astype(BF16), v_b[0:lo], preferred_element_type=F32)
        o_ref[0, lo:hi, :] = (o / l).astype(o_ref.dtype)


def _moba_prompt(q, k, v):
    b, h, s, d = q.shape
    assert s % MOBA_BLOCK == 0 and d == HEAD_DIM
    nb = s // MOBA_BLOCK
    assert nb % 8 == 0 and nb <= LANES
    onehot = jnp.asarray(np.arange(s)[:, None] // MOBA_BLOCK == np.arange(LANES)[None, :], BF16)
    head_map = lambda bi, hi: (bi, hi, 0, 0)
    return pl.pallas_call(
        functools.partial(_moba_prompt_kernel, n_blocks=nb),
        grid=(b, h),
        in_specs=[pl.BlockSpec((1, 1, s, d), head_map), pl.BlockSpec((1, 1, s, d), head_map),
                  pl.BlockSpec((1, 1, s, d), head_map), pl.BlockSpec((s, LANES), lambda bi, hi: (0, 0))],
        out_specs=pl.BlockSpec((1, s, d), lambda bi, hi: (bi, 0, hi)),
        out_shape=jax.ShapeDtypeStruct((b, s, h * d), BF16),
        compiler_params=_params("parallel", "parallel"),
        name="moba_prompt",
    )(q, k, v, onehot)


PAGE_GROUP = 8


def _moba_sample_probs_kernel(pt_ref, q_ref, kn_ref, *rest, n_pages, pages_per_block, group):
    kc_refs = rest[:group]
    p_ref, pown_ref, inv_ref, bsum_ref = rest[group:]
    step = pl.program_id(1)
    n_heads = q_ref.shape[1]
    n_tok = q_ref.shape[2]
    n_blocks = n_pages // pages_per_block
    scale = HEAD_DIM ** -0.5

    @pl.when(step == 0)
    def _():
        bsum_ref[...] = jnp.zeros_like(bsum_ref)

    page_sums = []
    for g in range(group):
        logits, sums = [], []
        for h in range(n_heads):
            kh = kc_refs[g][0, 0, h]
            sums.append(jnp.sum(kh, axis=0, keepdims=True))
            logits.append(lax.dot_general(q_ref[0, h].astype(BF16), kh.astype(BF16), NT_DIMS,
                                          preferred_element_type=F32) * scale)
        p_ref[0, step * group + g] = jnp.stack(logits)
        page_sums.append(sums)
    for bb in range(group // pages_per_block):
        blk = step * (group // pages_per_block) + bb
        for h in range(n_heads):
            tot = page_sums[bb * pages_per_block][h]
            for pp in range(1, pages_per_block):
                tot = tot + page_sums[bb * pages_per_block + pp][h]
            bsum_ref[h, pl.ds(blk, 1), :] = tot

    @pl.when(step == pl.num_programs(1) - 1)
    def _():
        sels, owns = [], []
        for h in range(n_heads):
            qh = q_ref[0, h]
            kmean = bsum_ref[h] / MOBA_BLOCK
            gate = lax.dot_general(qh, kmean, NT_DIMS, precision=HIGHEST, preferred_element_type=F32)
            lane = lax.broadcasted_iota(jnp.int32, gate.shape, 1)
            sels.append(_top_blocks(gate, lane < n_blocks, min(MOBA_TOPK, n_blocks), axis=1))
            kn = jnp.concatenate([kn_ref[0, h], jnp.zeros((LANES - n_tok, HEAD_DIM), F32)], axis=0).astype(BF16)
            own = lax.dot_general(qh.astype(BF16), kn, NT_DIMS, preferred_element_type=F32) * scale
            row = lax.broadcasted_iota(jnp.int32, own.shape, 0)
            owns.append(jnp.where(lane <= row, own, MASK_VALUE))
        sel = jnp.stack(sels)
        own = jnp.stack(owns)
        picked = [sel[:, :, b:b + 1] > 0.5 for b in range(n_blocks)]
        mrun = own
        for page in range(n_pages):
            mrun = jnp.maximum(mrun, jnp.where(picked[page // pages_per_block], p_ref[0, page], MASK_VALUE))
        m = jnp.max(mrun, axis=-1, keepdims=True)
        e_own = jnp.exp(own - m)
        lrun = e_own
        for page in range(n_pages):
            e = jnp.where(picked[page // pages_per_block], jnp.exp(p_ref[0, page] - m), 0.0)
            p_ref[0, page] = e
            lrun = lrun + e
        pown_ref[0] = e_own
        inv_ref[0] = jnp.broadcast_to(1.0 / jnp.sum(lrun, axis=-1, keepdims=True), inv_ref.shape[1:])


def _moba_sample_values_kernel(pt_ref, p_ref, pown_ref, inv_ref, vn_ref, *rest, group):
    vc_refs = rest[:group]
    o_ref = rest[group]
    step = pl.program_id(1)
    n_heads = vn_ref.shape[1]
    n_tok = vn_ref.shape[2]
    head = lambda h: slice(h * HEAD_DIM, (h + 1) * HEAD_DIM)

    @pl.when(step == 0)
    def _():
        for h in range(n_heads):
            pown = pown_ref[0, h]
            vn = vn_ref[0, h]
            own = pown[:, 0:1] * vn[0:1, :]
            for t in range(1, n_tok):
                own = own + pown[:, t:t + 1] * vn[t:t + 1, :]
            o_ref[0, :, head(h)] = own

    for h in range(n_heads):
        w = jnp.concatenate([p_ref[0, g, h] for g in range(group)], axis=1).astype(BF16)
        vals = jnp.concatenate([vc_refs[g][0, 0, h] for g in range(group)], axis=0).astype(BF16)
        o_ref[0, :, head(h)] += jnp.dot(w, vals, preferred_element_type=F32)

    @pl.when(step == pl.num_programs(1) - 1)
    def _():
        for h in range(n_heads):
            o_ref[0, :, head(h)] = o_ref[0, :, head(h)] * inv_ref[0, h]


def _moba_sample(q, k_new, v_new, cache_k, cache_v, page_table, layer):
    db, h, t, d = q.shape
    n_pages = page_table.shape[1]
    page = cache_k.shape[3]
    ppb = MOBA_BLOCK // page
    group = PAGE_GROUP
    assert page == LANES and d == HEAD_DIM and MOBA_BLOCK % page == 0
    assert (n_pages * page) % MOBA_BLOCK == 0 and t <= MOBA_BLOCK
    assert n_pages // ppb <= LANES and n_pages // ppb >= MOBA_TOPK
    assert n_pages % group == 0 and group % ppb == 0
    n_steps = n_pages // group
    pt = page_table.reshape(-1).astype(jnp.int32)

    seq4 = lambda di, si, pt_ref: (di, 0, 0, 0)
    cache_specs = [
        pl.BlockSpec((1, 1, h, page, d),
                     lambda di, si, pt_ref, g=g: (layer, pt_ref[di * n_pages + si * group + g], 0, 0, 0))
        for g in range(group)]

    probs, p_own, inv = pl.pallas_call(
        functools.partial(_moba_sample_probs_kernel, n_pages=n_pages, pages_per_block=ppb, group=group),
        grid_spec=pltpu.PrefetchScalarGridSpec(
            num_scalar_prefetch=1,
            grid=(db, n_steps),
            in_specs=[pl.BlockSpec((1, h, t, d), seq4), pl.BlockSpec((1, h, t, d), seq4)] + cache_specs,
            out_specs=[pl.BlockSpec((1, n_pages, h, t, page), lambda di, si, pt_ref: (di, 0, 0, 0, 0)),
                       pl.BlockSpec((1, h, t, LANES), seq4), pl.BlockSpec((1, h, t, LANES), seq4)],
            scratch_shapes=[pltpu.VMEM((h, LANES, d), F32)],
        ),
        out_shape=[jax.ShapeDtypeStruct((db, n_pages, h, t, page), F32),
                   jax.ShapeDtypeStruct((db, h, t, LANES), F32),
                   jax.ShapeDtypeStruct((db, h, t, LANES), F32)],
        compiler_params=_params("parallel", "arbitrary"),
        name="moba_sample_probs",
    )(pt, q, k_new, *([cache_k] * group))

    return pl.pallas_call(
        functools.partial(_moba_sample_values_kernel, group=group),
        grid_spec=pltpu.PrefetchScalarGridSpec(
            num_scalar_prefetch=1,
            grid=(db, n_steps),
            in_specs=[pl.BlockSpec((1, group, h, t, page), lambda di, si, pt_ref: (di, si, 0, 0, 0)),
                      pl.BlockSpec((1, h, t, LANES), seq4), pl.BlockSpec((1, h, t, LANES), seq4),
                      pl.BlockSpec((1, h, t, d), seq4)] + cache_specs,
            out_specs=pl.BlockSpec((1, t, h * d), lambda di, si, pt_ref: (di, 0, 0)),
        ),
        out_shape=jax.ShapeDtypeStruct((db, t, h * d), F32),
        compiler_params=_params("parallel", "arbitrary"),
        name="moba_sample_values",
    )(pt, probs, p_own, inv, v_new, *([cache_v] * group))


def _gla_tables(c, key_rows):
    widths = []
    w = c
    while w >= 2:
        widths.append(w)
        w //= 2
    t = np.arange(c)
    prefix = [np.tril(np.ones((c, c), np.float32))]
    masks = [np.eye(c, dtype=np.float32)]
    for w in widths:
        base = (t // w) * w
        mid = base + w // 2
        upper = t >= mid
        pm = np.zeros((c, c), np.float32)
        for r in range(c):
            if upper[r]:
                pm[r, mid[r]:r + 1] = 1.0
            else:
                pm[r, r + 1:mid[r]] = 1.0
        prefix.append(pm)
        same = base[:, None] == base[None, :]
        masks.append((same & upper[:, None] & (~upper)[None, :]).astype(np.float32))
    pad = ((0, 0), (0, key_rows - c))
    return np.pad(np.concatenate(prefix, axis=0), pad), np.pad(np.stack(masks, axis=0), ((0, 0),) + pad)


def _pad_rows(x, n):
    if x.shape[0] == n:
        return x
    return jnp.concatenate([x, jnp.zeros((n - x.shape[0], x.shape[1]), x.dtype)], axis=0)


def _gla_chunk(q, k, v_f32, a, pre, msk_ref, state_t):
    n_levels, c, kr = msk_ref.shape
    v = v_f32.astype(BF16)
    a1 = a.astype(BF16)
    a2 = (a - a1.astype(F32)).astype(BF16)
    sums = jnp.dot(pre, a1, preferred_element_type=F32) + jnp.dot(pre, a2, preferred_element_type=F32)

    cum = sums[0:c]
    o = lax.dot_general((q * jnp.exp(cum)).astype(BF16), state_t.astype(BF16), NT_DIMS,
                        preferred_element_type=F32)
    scores = lax.dot_general(q.astype(BF16), _pad_rows(k, kr).astype(BF16), NT_DIMS,
                             preferred_element_type=F32) * msk_ref[0]
    for lv in range(1, n_levels):
        e = jnp.exp(sums[lv * c:(lv + 1) * c])
        s_lv = lax.dot_general((q * e).astype(BF16), _pad_rows(k * e, kr).astype(BF16), NT_DIMS,
                               preferred_element_type=F32)
        scores = scores + s_lv * msk_ref[lv]
    o = o + jnp.dot(scores.astype(BF16), v, preferred_element_type=F32)

    last = cum[c - 1:c]
    k_dec = _pad_rows(k * jnp.exp(last - cum), kr).astype(BF16)
    upd_t = jnp.dot(v_f32.T.astype(BF16), k_dec, preferred_element_type=F32)
    return o, jnp.exp(last) * state_t + upd_t


def _gla_kernel(*refs, has_state0, heads_per_step):
    if has_state0:
        q_ref, k_ref, v_ref, og_ref, a_ref, pre_ref, msk_ref, g_ref, s0_ref, y_ref, sout_ref, st_ref = refs
    else:
        q_ref, k_ref, v_ref, og_ref, a_ref, pre_ref, msk_ref, g_ref, y_ref, sout_ref, st_ref = refs
        s0_ref = None
    ci = pl.program_id(2)
    _, c, kr = msk_ref.shape
    n_rows = q_ref.shape[0]
    gk, gv = GLA_KEY_DIM, GLA_VAL_DIM

    @pl.when(ci == 0)
    def _():
        for hp in range(heads_per_step):
            if has_state0:
                st_ref[hp] = s0_ref[0, hp].T
            else:
                st_ref[hp] = jnp.zeros(st_ref.shape[1:], F32)

    pre = pre_ref[...]
    for hp in range(heads_per_step):
        kcols = slice(hp * gk, (hp + 1) * gk)
        vcols = slice(hp * gv, (hp + 1) * gv)
        q = _pad_rows(q_ref[:, kcols], c) * (GLA_KEY_DIM ** -0.5)
        k = _pad_rows(k_ref[:, kcols], c)
        v_f32 = _pad_rows(v_ref[:, vcols], kr)
        a = _pad_rows(a_ref[:, kcols], kr)
        o, new_state_t = _gla_chunk(q, k, v_f32, a, pre, msk_ref, st_ref[hp])
        st_ref[hp] = new_state_t

        @pl.when(ci == pl.num_programs(2) - 1)
        def _(hp=hp, new_state_t=new_state_t):
            sout_ref[0, hp] = new_state_t.T

        o = o[0:n_rows]
        ms = jnp.mean(o * o, axis=-1, keepdims=True)
        og = og_ref[:, vcols]
        y = (o * lax.rsqrt(ms + NORM_EPS) * g_ref[...]) * (og * _sigmoid(og))
        y_ref[:, vcols] = y.astype(y_ref.dtype)


GLA_HEADS_PER_STEP = 4


def _gla(u, log_a, gla_norm_g, state0, nseq, seqlen):
    m = u.shape[0]
    n_rows = min(GLA_CHUNK, seqlen)
    assert seqlen % n_rows == 0 and n_rows % 8 == 0
    n_chunks = seqlen // n_rows
    c = max(GLA_SHORT_CHUNK, n_rows)
    assert c & (c - 1) == 0 and c <= GLA_CHUNK
    pre_np, msk_np = _gla_tables(c, GLA_CHUNK)
    pre = jnp.asarray(pre_np, BF16)
    msk = jnp.asarray(msk_np, F32)
    gh, gk, gv = GLA_HEADS, GLA_KEY_DIM, GLA_VAL_DIM
    hps = GLA_HEADS_PER_STEP
    assert gh % hps == 0
    ng = gh // hps
    kw, vw = hps * gk, hps * gv
    k0, v0, og0 = (gh * gk) // kw, (2 * gh * gk) // vw, (2 * gh * gk + gh * gv) // vw
    assert k0 * kw == gh * gk and v0 * vw == 2 * gh * gk and og0 * vw == 2 * gh * gk + gh * gv
    row = lambda b, ci: b * n_chunks + ci
    in_specs = [
        pl.BlockSpec((n_rows, kw), lambda b, hg, ci: (row(b, ci), hg)),
        pl.BlockSpec((n_rows, kw), lambda b, hg, ci: (row(b, ci), k0 + hg)),
        pl.BlockSpec((n_rows, vw), lambda b, hg, ci: (row(b, ci), v0 + hg)),
        pl.BlockSpec((n_rows, vw), lambda b, hg, ci: (row(b, ci), og0 + hg)),
        pl.BlockSpec((n_rows, kw), lambda b, hg, ci: (row(b, ci), hg)),
        pl.BlockSpec(pre.shape, lambda b, hg, ci: (0, 0)),
        pl.BlockSpec(msk.shape, lambda b, hg, ci: (0, 0, 0)),
        pl.BlockSpec((1, gv), lambda b, hg, ci: (0, 0)),
    ]
    args = [u, u, u, u, log_a, pre, msk, gla_norm_g.reshape(1, gv)]
    if state0 is not None:
        in_specs.append(pl.BlockSpec((1, hps, gk, gv), lambda b, hg, ci: (b, hg, 0, 0)))
        args.append(state0)
    kern = functools.partial(_gla_kernel, has_state0=state0 is not None, heads_per_step=hps)
    return pl.pallas_call(
        kern,
        grid=(nseq, ng, n_chunks),
        in_specs=in_specs,
        out_specs=[pl.BlockSpec((n_rows, vw), lambda b, hg, ci: (row(b, ci), hg)),
                   pl.BlockSpec((1, hps, gk, gv), lambda b, hg, ci: (b, hg, 0, 0))],
        out_shape=[jax.ShapeDtypeStruct((m, gh * gv), BF16 if n_rows % 16 == 0 else F32),
                   jax.ShapeDtypeStruct((nseq, gh, gk, gv), F32)],
        scratch_shapes=[pltpu.VMEM((hps, gv, gk), F32)],
        compiler_params=_params("parallel", "parallel", "arbitrary"),
        name="gla",
    )(*args)


ROW_TILE = 512
WIDE_ROW_TILE = 1024
COL_TILE = 1024
MERGE_COL_TILE = 512


def _tile(m, pref):
    t = min(m, pref)
    assert m % t == 0
    return t


def _run_group(x, attend, state0, wts):
    nseq, seqlen, d = x.shape
    m = nseq * seqlen
    x2 = x.reshape(m, d)
    tm = _tile(m, ROW_TILE)
    tm_wide = _tile(m, WIDE_ROW_TILE)

    h = _rmsnorm(x2, wts["norm1_g"], tm)
    tmh = _tile(seqlen, WIDE_ROW_TILE) if seqlen >= 256 else m
    w_in, aw, nh = wts["w_in"], ATTN_HEADS * HEAD_DIM, ATTN_HEADS
    hpt = COL_TILE // HEAD_DIM
    q = _proj_heads(h, w_in, 0, nh, wts["q_norm_g"], nseq, seqlen, True, tmh, hpt, "proj_q")
    k = _proj_heads(h, w_in, aw, nh, wts["k_norm_g"], nseq, seqlen, True, tmh, hpt, "proj_k")
    v = _proj_heads(h, w_in, 2 * aw, nh, wts["k_norm_g"], nseq, seqlen, False, tmh, hpt, "proj_v")
    gla_cols = 2 * GLA_HEADS * (GLA_KEY_DIM + GLA_VAL_DIM)
    u = _matmul(h, w_in, 3 * aw, gla_cols, F32, tm_wide, COL_TILE, "proj_gla")
    log_a = _loga(h, wts["w_ag"], wts["w_a2"], wts["b_a"], tm)

    y_attn = attend(q, k, v)
    y_gla, gla_state = _gla(u, log_a, wts["gla_norm_g"], state0, nseq, seqlen)

    mixed = _merge(h, y_attn.reshape(m, -1), y_gla, wts["w_gates"],
                   wts["w_branch_attn"], wts["w_branch_gla"], tm, MERGE_COL_TILE)
    x1 = _matmul_residual(mixed, wts["w_out"], x2, tm_wide, COL_TILE, "out_proj")
    y = _mlp(x1, wts["norm2_g"], wts["w_up"], wts["w_down"], tm, COL_TILE)
    return y.reshape(nseq, seqlen, d), k, v, gla_state


def kernel(x_prompt, x_sample, cache_k, cache_v, state_gla, page_table, norm1_g, w_in, q_norm_g, k_norm_g,
           w_a2, b_a, gla_norm_g, w_branch_attn, w_branch_gla, w_out, norm2_g, w_up, w_down):
    depth = w_in.shape[0]
    d_model = x_prompt.shape[-1]
    ag0 = 3 * ATTN_HEADS * HEAD_DIM + 2 * GLA_HEADS * (GLA_KEY_DIM + GLA_VAL_DIM)

    yp, ys = x_prompt, x_sample
    outs = [[] for _ in range(6)]
    for l in range(depth):
        w_l = w_in[l].astype(BF16)
        assert w_l.shape[1] == ag0 + GLA_GATE_RANK + 2 * d_model
        wts = {
            "norm1_g": norm1_g[l], "q_norm_g": q_norm_g[l], "k_norm_g": k_norm_g[l],
            "w_in": w_l,
            "w_ag": jnp.pad(w_l[:, ag0:ag0 + GLA_GATE_RANK], ((0, 0), (0, LANES - GLA_GATE_RANK))),
            "w_gates": w_l[:, ag0 + GLA_GATE_RANK:],
            "w_a2": jnp.pad(w_a2[l], ((0, LANES - GLA_GATE_RANK), (0, 0))),
            "b_a": b_a[l].reshape(1, -1), "gla_norm_g": gla_norm_g[l],
            "w_branch_attn": w_branch_attn[l].astype(BF16), "w_branch_gla": w_branch_gla[l].astype(BF16),
            "w_out": w_out[l].astype(BF16), "norm2_g": norm2_g[l],
            "w_up": w_up[l].astype(BF16), "w_down": w_down[l].astype(BF16),
        }
        yp, k_l, v_l, s_l = _run_group(yp, _moba_prompt, None, wts)
        outs[0].append(k_l); outs[1].append(v_l); outs[2].append(s_l)
        attend_s = lambda q, k, v: _moba_sample(q, k, v, cache_k, cache_v, page_table, l)
        ys, k_l, v_l, s_l = _run_group(ys, attend_s, state_gla[l], wts)
        outs[3].append(k_l); outs[4].append(v_l); outs[5].append(s_l)
    stack = lambda xs: xs[0][None] if len(xs) == 1 else jnp.stack(xs)
    return (yp, ys) + tuple(stack(o) for o in outs)
```

```python
import functools

import numpy as np
import jax
import jax.numpy as jnp
from jax import lax
from jax.experimental import pallas as pl
from jax.experimental.pallas import tpu as pltpu

F32 = jnp.float32
BF16 = jnp.bfloat16
HIGHEST = lax.Precision.HIGHEST

NORM_EPS = 1e-6
ATTN_HEADS = 16
HEAD_DIM = 128
MOBA_BLOCK = 256
MOBA_TOPK = 3
GLA_HEADS = 4
GLA_KEY_DIM = 256
GLA_VAL_DIM = 512
GLA_GATE_RANK = 16
GLA_GATE_NORMALIZER = 16.0

LANES = 128
VMEM_LIMIT_BYTES = 56 * 1024 * 1024
GLA_CHUNK = 128
GLA_SHORT_CHUNK = 16
MASK_VALUE = -1e30
LOG2_E = 1.4426950408889634

NT_DIMS = (((1,), (1,)), ((), ()))
TN_DIMS = (((0,), (0,)), ((), ()))


def _params(*sem):
    return pltpu.CompilerParams(dimension_semantics=sem, vmem_limit_bytes=VMEM_LIMIT_BYTES)


def _sigmoid(x):
    return 1.0 / (1.0 + jnp.exp(-x))


def _rmsnorm_kernel(x_ref, g_ref, o_ref):
    x = x_ref[...]
    ms = jnp.mean(x * x, axis=-1, keepdims=True)
    o_ref[...] = (x * lax.rsqrt(ms + NORM_EPS) * g_ref[...]).astype(o_ref.dtype)


def _rmsnorm(x, g, tm):
    m, d = x.shape
    return pl.pallas_call(
        _rmsnorm_kernel,
        grid=(m // tm,),
        in_specs=[pl.BlockSpec((tm, d), lambda i: (i, 0)), pl.BlockSpec((1, d), lambda i: (0, 0))],
        out_specs=pl.BlockSpec((tm, d), lambda i: (i, 0)),
        out_shape=jax.ShapeDtypeStruct((m, d), BF16),
        compiler_params=_params("parallel"),
        name="rmsnorm",
    )(x, g.reshape(1, d))


def _matmul_kernel(a_ref, w_ref, o_ref):
    o_ref[...] = jnp.dot(a_ref[...], w_ref[...], preferred_element_type=F32).astype(o_ref.dtype)


def _matmul(a, w, col0, n, out_dtype, tm, tn, name):
    m, k = a.shape
    assert col0 % tn == 0 and n % tn == 0
    return pl.pallas_call(
        _matmul_kernel,
        grid=(m // tm, n // tn),
        in_specs=[pl.BlockSpec((tm, k), lambda i, j: (i, 0)),
                  pl.BlockSpec((k, tn), lambda i, j: (0, col0 // tn + j))],
        out_specs=pl.BlockSpec((tm, tn), lambda i, j: (i, j)),
        out_shape=jax.ShapeDtypeStruct((m, n), out_dtype),
        compiler_params=_params("parallel", "parallel"),
        name=name,
    )(a, w)


def _matmul_residual_kernel(a_ref, w_ref, r_ref, o_ref):
    o_ref[...] = r_ref[...] + jnp.dot(a_ref[...], w_ref[...], preferred_element_type=F32)


def _matmul_residual(a, w, r, tm, tn, name):
    m, k = a.shape
    n = w.shape[1]
    return pl.pallas_call(
        _matmul_residual_kernel,
        grid=(m // tm, n // tn),
        in_specs=[pl.BlockSpec((tm, k), lambda i, j: (i, 0)), pl.BlockSpec((k, tn), lambda i, j: (0, j)),
                  pl.BlockSpec((tm, tn), lambda i, j: (i, j))],
        out_specs=pl.BlockSpec((tm, tn), lambda i, j: (i, j)),
        out_shape=jax.ShapeDtypeStruct((m, n), F32),
        compiler_params=_params("parallel", "parallel"),
        name=name,
    )(a, w, r)


def _proj_heads_kernel(h_ref, w_ref, g_ref, o_ref, *, normalize, heads_per_tile, seq_major):
    res = jnp.dot(h_ref[...], w_ref[...], preferred_element_type=F32)
    for i in range(heads_per_tile):
        r = res[:, i * HEAD_DIM:(i + 1) * HEAD_DIM]
        if normalize:
            ms = jnp.mean(r * r, axis=-1, keepdims=True)
            r = r * lax.rsqrt(ms + NORM_EPS) * g_ref[...]
        if seq_major:
            o_ref[0, i] = r
        else:
            o_ref[:, i] = r.reshape(o_ref.shape[0], o_ref.shape[2], HEAD_DIM)


def _proj_heads(h, w, col0, nh, g, nseq, seqlen, normalize, tm, heads_per_tile, name):
    m, d = h.shape
    tn = heads_per_tile * HEAD_DIM
    assert col0 % tn == 0 and nh % heads_per_tile == 0
    ct0 = col0 // tn
    out_shape = jax.ShapeDtypeStruct((nseq, nh, seqlen, HEAD_DIM), F32)
    gain = g.reshape(1, HEAD_DIM)
    if seqlen % tm == 0:
        spt = seqlen // tm
        kern = functools.partial(_proj_heads_kernel, normalize=normalize, heads_per_tile=heads_per_tile, seq_major=True)
        return pl.pallas_call(
            kern,
            grid=(nseq, spt, nh // heads_per_tile),
            in_specs=[pl.BlockSpec((tm, d), lambda b, s, j: (b * spt + s, 0)),
                      pl.BlockSpec((d, tn), lambda b, s, j: (0, ct0 + j)),
                      pl.BlockSpec((1, HEAD_DIM), lambda b, s, j: (0, 0))],
            out_specs=pl.BlockSpec((1, heads_per_tile, tm, HEAD_DIM), lambda b, s, j: (b, j, s, 0)),
            out_shape=out_shape,
            compiler_params=_params("parallel", "parallel", "parallel"),
            name=name,
        )(h, w, gain)
    assert tm == m
    kern = functools.partial(_proj_heads_kernel, normalize=normalize, heads_per_tile=heads_per_tile, seq_major=False)
    return pl.pallas_call(
        kern,
        grid=(nh // heads_per_tile,),
        in_specs=[pl.BlockSpec((m, d), lambda j: (0, 0)),
                  pl.BlockSpec((d, tn), lambda j: (0, ct0 + j)),
                  pl.BlockSpec((1, HEAD_DIM), lambda j: (0, 0))],
        out_specs=pl.BlockSpec((nseq, heads_per_tile, seqlen, HEAD_DIM), lambda j: (0, j, 0, 0)),
        out_shape=out_shape,
        compiler_params=_params("parallel"),
        name=name,
    )(h, w, gain)


def _loga_kernel(h_ref, wag_ref, wa2_ref, ba_ref, o_ref):
    ag = jnp.dot(h_ref[...], wag_ref[...], preferred_element_type=F32)
    z = jnp.dot(ag, wa2_ref[...], preferred_element_type=F32, precision=HIGHEST) + ba_ref[...]
    ls = jnp.minimum(z, 0.0) - jnp.log1p(jnp.exp(-jnp.abs(z)))
    o_ref[...] = ls * (1.0 / GLA_GATE_NORMALIZER)


def _loga(h, wag, wa2, ba, tm):
    m, d = h.shape
    n = wa2.shape[1]
    return pl.pallas_call(
        _loga_kernel,
        grid=(m // tm,),
        in_specs=[pl.BlockSpec((tm, d), lambda i: (i, 0)), pl.BlockSpec((d, LANES), lambda i: (0, 0)),
                  pl.BlockSpec((LANES, n), lambda i: (0, 0)), pl.BlockSpec((1, n), lambda i: (0, 0))],
        out_specs=pl.BlockSpec((tm, n), lambda i: (i, 0)),
        out_shape=jax.ShapeDtypeStruct((m, n), F32),
        compiler_params=_params("parallel"),
        name="gla_log_alpha",
    )(h, wag, wa2, ba)


def _merge_kernel(h_ref, ya_ref, yg_ref, wga_ref, wgg_ref, wba_ref, wbg_ref, o_ref):
    h = h_ref[...]
    ga = jnp.dot(h, wga_ref[...], preferred_element_type=F32)
    gg = jnp.dot(h, wgg_ref[...], preferred_element_type=F32)
    ba = jnp.dot(ya_ref[...].astype(BF16), wba_ref[...], preferred_element_type=F32)
    bg = jnp.dot(yg_ref[...].astype(BF16), wbg_ref[...], preferred_element_type=F32)
    o_ref[...] = (_sigmoid(ga) * ba + _sigmoid(gg) * bg).astype(o_ref.dtype)


def _merge(h, y_attn, y_gla, w_gates, wba, wbg, tm, tn):
    m, d = h.shape
    n = wba.shape[1]
    assert w_gates.shape[1] == 2 * n and n % tn == 0
    row = lambda i, j: (i, 0)
    col = lambda i, j: (0, j)
    return pl.pallas_call(
        _merge_kernel,
        grid=(m // tm, n // tn),
        in_specs=[pl.BlockSpec((tm, d), row), pl.BlockSpec((tm, y_attn.shape[1]), row),
                  pl.BlockSpec((tm, y_gla.shape[1]), row),
                  pl.BlockSpec((d, tn), col), pl.BlockSpec((d, tn), lambda i, j: (0, n // tn + j)),
                  pl.BlockSpec((wba.shape[0], tn), col), pl.BlockSpec((wbg.shape[0], tn), col)],
        out_specs=pl.BlockSpec((tm, tn), lambda i, j: (i, j)),
        out_shape=jax.ShapeDtypeStruct((m, n), BF16),
        compiler_params=_params("parallel", "parallel"),
        name="branch_merge",
    )(h, y_attn, y_gla, w_gates, w_gates, wba, wbg)


def _mlp_kernel(x_ref, g_ref, wup_ref, wdn_ref, o_ref, h_ref, acc_ref):
    f = pl.program_id(1)

    @pl.when(f == 0)
    def _():
        x = x_ref[...]
        ms = jnp.mean(x * x, axis=-1, keepdims=True)
        h_ref[...] = (x * lax.rsqrt(ms + NORM_EPS) * g_ref[...]).astype(BF16)
        acc_ref[...] = jnp.zeros_like(acc_ref)

    u = jnp.dot(h_ref[...], wup_ref[...], preferred_element_type=F32)
    a = jnp.square(jnp.maximum(u, 0.0)).astype(BF16)
    acc_ref[...] += jnp.dot(a, wdn_ref[...], preferred_element_type=F32)

    @pl.when(f == pl.num_programs(1) - 1)
    def _():
        o_ref[...] = x_ref[...] + acc_ref[...]


def _mlp(x, g, wup, wdn, tm, tf):
    m, d = x.shape
    ff = wup.shape[1]
    return pl.pallas_call(
        _mlp_kernel,
        grid=(m // tm, ff // tf),
        in_specs=[pl.BlockSpec((tm, d), lambda i, f: (i, 0)), pl.BlockSpec((1, d), lambda i, f: (0, 0)),
                  pl.BlockSpec((d, tf), lambda i, f: (0, f)), pl.BlockSpec((tf, d), lambda i, f: (f, 0))],
        out_specs=pl.BlockSpec((tm, d), lambda i, f: (i, 0)),
        out_shape=jax.ShapeDtypeStruct((m, d), F32),
        scratch_shapes=[pltpu.VMEM((tm, d), BF16), pltpu.VMEM((tm, d), F32)],
        compiler_params=_params("parallel", "arbitrary"),
        name="mlp",
    )(x, g.reshape(1, d), wup, wdn)


def _top_blocks(scores, valid, n_sel, axis):
    idxf = lax.broadcasted_iota(jnp.int32, scores.shape, axis).astype(F32)
    sv = jnp.where(valid, scores, -jnp.inf)
    sel = jnp.zeros(scores.shape, F32)
    for _ in range(n_sel):
        mx = jnp.max(sv, axis=axis, keepdims=True)
        idx = jnp.min(jnp.where(sv == mx, idxf, float(LANES)), axis=axis, keepdims=True)
        pick = idxf == idx
        sel = jnp.where(pick, 1.0, sel)
        sv = jnp.where(pick, -jnp.inf, sv)
    return jnp.where(valid, sel, 0.0)


def _moba_prompt_kernel(q_ref, k_ref, v_ref, onehot_ref, o_ref, *, n_blocks):
    scale = HEAD_DIM ** -0.5
    blk = MOBA_BLOCK
    q = q_ref[0, 0]
    k = k_ref[0, 0]
    seq = q.shape[0]
    kmean = jnp.concatenate(
        [jnp.sum(k[j * blk:(j + 1) * blk], axis=0, keepdims=True) / blk for j in range(n_blocks)], axis=0)
    gate_t = lax.dot_general(kmean, q, NT_DIMS, precision=HIGHEST, preferred_element_type=F32)
    key_blk = lax.broadcasted_iota(jnp.int32, gate_t.shape, 0)
    qry_blk = lax.broadcasted_iota(jnp.int32, gate_t.shape, 1) // blk
    past = key_blk < qry_blk
    sel_t = _top_blocks(gate_t, past, min(MOBA_TOPK, n_blocks - 1), axis=0)
    allowed = (sel_t > 0.5) | (key_blk == qry_blk)
    mask_t = jnp.where(allowed, 0.0, MASK_VALUE)
    mask_t = jnp.concatenate([mask_t, jnp.zeros((LANES - n_blocks, seq), F32)], axis=0).astype(BF16)

    row = lax.broadcasted_iota(jnp.int32, (blk, blk), 0)
    col = lax.broadcasted_iota(jnp.int32, (blk, blk), 1)
    eye = jnp.where(row == col, 1.0, 0.0).astype(BF16)
    k_aug = jnp.concatenate([k.astype(BF16), onehot_ref[...]], axis=1)
    v_b = v_ref[0, 0].astype(BF16)

    for i in range(n_blocks):
        lo, hi = i * blk, (i + 1) * blk
        mask_i = lax.dot_general(eye, mask_t[:, lo:hi], NT_DIMS, preferred_element_type=F32)
        q_aug = jnp.concatenate([q[lo:hi].astype(BF16), mask_i.astype(BF16)], axis=1)
        s = lax.dot_general(q_aug, k_aug[0:hi], NT_DIMS, preferred_element_type=F32) * (scale * LOG2_E)
        s_own = jnp.where(col <= row, s[:, lo:hi], MASK_VALUE)
        m = jnp.max(s_own, axis=-1, keepdims=True)
        if i > 0:
            s_past = s[:, 0:lo]
            m = jnp.maximum(m, jnp.max(s_past, axis=-1, keepdims=True))
            p_past = jnp.exp2(s_past - m)
        p_own = jnp.exp2(s_own - m)
        l = jnp.sum(p_own, axis=-1, keepdims=True)
        o = jnp.dot(p_own.astype(BF16), v_b[lo:hi], preferred_element_type=F32)
        if i > 0:
            l = l + jnp.sum(p_past, axis=-1, keepdims=True)
            o = o + jnp.dot(p_past.astype(BF16), v_b[0:lo], preferred_element_type=F32)
        o_ref[0, lo:hi, :] = (o / l).astype(o_ref.dtype)


def _moba_prompt(q, k, v):
    b, h, s, d = q.shape
    assert s % MOBA_BLOCK == 0 and d == HEAD_DIM
    nb = s // MOBA_BLOCK
    assert nb % 8 == 0 and nb <= LANES
    onehot = jnp.asarray(np.arange(s)[:, None] // MOBA_BLOCK == np.arange(LANES)[None, :], BF16)
    head_map = lambda bi, hi: (bi, hi, 0, 0)
    return pl.pallas_call(
        functools.partial(_moba_prompt_kernel, n_blocks=nb),
        grid=(b, h),
        in_specs=[pl.BlockSpec((1, 1, s, d), head_map), pl.BlockSpec((1, 1, s, d), head_map),
                  pl.BlockSpec((1, 1, s, d), head_map), pl.BlockSpec((s, LANES), lambda bi, hi: (0, 0))],
        out_specs=pl.BlockSpec((1, s, d), lambda bi, hi: (bi, 0, hi)),
        out_shape=jax.ShapeDtypeStruct((b, s, h * d), BF16),
        compiler_params=_params("parallel", "parallel"),
        name="moba_prompt",
    )(q, k, v, onehot)


PAGE_GROUP = 8


def _own_head_lanes(x, n_heads, rows):
    lane_head = lax.broadcasted_iota(jnp.int32, (rows, LANES), 1) // (LANES // n_heads)
    out = x[0:rows]
    for h in range(1, n_heads):
        out = jnp.where(lane_head == h, x[h * rows:(h + 1) * rows], out)
    return out


def _moba_sample_probs_kernel(pt_ref, q_ref, kn_ref, *rest, n_pages, pages_per_block, group):
    kc_refs = rest[:group]
    p_ref, pown_ref, inv_ref, lt_ref, bsum_ref = rest[group:]
    step = pl.program_id(1)
    n_heads, n_tok, d = q_ref.shape[1:]
    page = kc_refs[0].shape[3]
    n_blocks = n_pages // pages_per_block
    scale = HEAD_DIM ** -0.5
    q_all = q_ref[0].reshape(n_heads * n_tok, d)
    q_all_b = q_all.astype(BF16)

    page_sums = []
    for g in range(group):
        keys = kc_refs[g][0, 0]
        page_sums.append(jnp.sum(keys, axis=1))
        full = lax.dot_general(keys.reshape(n_heads * page, d).astype(BF16), q_all_b, NT_DIMS,
                               preferred_element_type=F32)
        lt_ref[step * group + g] = _own_head_lanes(full, n_heads, page) * scale
    for bb in range(group // pages_per_block):
        tot = page_sums[bb * pages_per_block]
        for pp in range(1, pages_per_block):
            tot = tot + page_sums[bb * pages_per_block + pp]
        bsum_ref[step * (group // pages_per_block) + bb] = tot

    @pl.when(step == pl.num_programs(1) - 1)
    def _():
        lane = lax.broadcasted_iota(jnp.int32, (LANES, LANES), 1)
        row = lax.broadcasted_iota(jnp.int32, (LANES, LANES), 0)
        kmean = [jnp.concatenate([bsum_ref[b][h:h + 1] for b in range(n_blocks)], axis=0) / MOBA_BLOCK
                 for h in range(n_heads)]
        gate_full = lax.dot_general(jnp.concatenate(kmean, axis=0), q_all, NT_DIMS, precision=HIGHEST,
                                    preferred_element_type=F32)
        gate_t = _own_head_lanes(gate_full, n_heads, n_blocks)
        sel_t = _top_blocks(gate_t, row[0:n_blocks] >= 0, min(MOBA_TOPK, n_blocks), axis=0)
        kn_all = kn_ref[0].reshape(n_heads * n_tok, d).astype(BF16)
        own_full = lax.dot_general(kn_all, q_all_b, NT_DIMS, preferred_element_type=F32)
        own_t = _own_head_lanes(own_full, n_heads, n_tok) * scale
        own_t = jnp.where(row[0:n_tok] <= lane[0:n_tok] % n_tok, own_t, MASK_VALUE)

        picked = [sel_t[b:b + 1] > 0.5 for b in range(n_blocks)]
        mrun = jnp.full((page, LANES), MASK_VALUE, F32)
        for pg in range(n_pages):
            mrun = jnp.maximum(mrun, jnp.where(picked[pg // pages_per_block], lt_ref[pg], MASK_VALUE))
        m = jnp.maximum(jnp.max(mrun, axis=0, keepdims=True), jnp.max(own_t, axis=0, keepdims=True))
        e_own = jnp.exp(own_t - m)
        lrun = jnp.zeros((page, LANES), F32)
        for pg in range(n_pages):
            e = jnp.where(picked[pg // pages_per_block], jnp.exp(lt_ref[pg] - m), 0.0)
            lrun = lrun + e
            p_ref[0, pg] = e.T.reshape(n_heads, n_tok, page)
        total = jnp.sum(lrun, axis=0, keepdims=True) + jnp.sum(e_own, axis=0, keepdims=True)
        e_own_pad = jnp.concatenate([e_own, jnp.zeros((LANES - n_tok, LANES), F32)], axis=0)
        pown_ref[0] = e_own_pad.T.reshape(n_heads, n_tok, LANES)
        inv_ref[0] = jnp.broadcast_to(1.0 / total, (LANES, LANES)).T.reshape(n_heads, n_tok, LANES)


def _moba_sample_values_kernel(pt_ref, p_ref, pown_ref, inv_ref, vn_ref, *rest, group):
    vc_refs = rest[:group]
    o_ref = rest[group]
    step = pl.program_id(1)
    n_heads = vn_ref.shape[1]
    n_tok = vn_ref.shape[2]
    head = lambda h: slice(h * HEAD_DIM, (h + 1) * HEAD_DIM)

    @pl.when(step == 0)
    def _():
        for h in range(n_heads):
            pown = pown_ref[0, h]
            vn = vn_ref[0, h]
            own = pown[:, 0:1] * vn[0:1, :]
            for t in range(1, n_tok):
                own = own + pown[:, t:t + 1] * vn[t:t + 1, :]
            o_ref[0, :, head(h)] = own

    for h in range(n_heads):
        w = jnp.concatenate([p_ref[0, g, h] for g in range(group)], axis=1).astype(BF16)
        vals = jnp.concatenate([vc_refs[g][0, 0, h] for g in range(group)], axis=0).astype(BF16)
        o_ref[0, :, head(h)] += jnp.dot(w, vals, preferred_element_type=F32)

    @pl.when(step == pl.num_programs(1) - 1)
    def _():
        for h in range(n_heads):
            o_ref[0, :, head(h)] = o_ref[0, :, head(h)] * inv_ref[0, h]


def _moba_sample(q, k_new, v_new, cache_k, cache_v, page_table, layer):
    db, h, t, d = q.shape
    n_pages = page_table.shape[1]
    page = cache_k.shape[3]
    ppb = MOBA_BLOCK // page
    group = PAGE_GROUP
    assert page == LANES and d == HEAD_DIM and MOBA_BLOCK % page == 0
    assert (n_pages * page) % MOBA_BLOCK == 0 and t <= MOBA_BLOCK
    assert n_pages // ppb <= LANES and n_pages // ppb >= MOBA_TOPK
    assert n_pages % group == 0 and group % ppb == 0 and h * t == LANES and (n_pages // ppb) % 8 == 0
    n_steps = n_pages // group
    pt = page_table.reshape(-1).astype(jnp.int32)

    seq4 = lambda di, si, pt_ref: (di, 0, 0, 0)
    cache_specs = [
        pl.BlockSpec((1, 1, h, page, d),
                     lambda di, si, pt_ref, g=g: (layer, pt_ref[di * n_pages + si * group + g], 0, 0, 0))
        for g in range(group)]

    probs, p_own, inv = pl.pallas_call(
        functools.partial(_moba_sample_probs_kernel, n_pages=n_pages, pages_per_block=ppb, group=group),
        grid_spec=pltpu.PrefetchScalarGridSpec(
            num_scalar_prefetch=1,
            grid=(db, n_steps),
            in_specs=[pl.BlockSpec((1, h, t, d), seq4), pl.BlockSpec((1, h, t, d), seq4)] + cache_specs,
            out_specs=[pl.BlockSpec((1, n_pages, h, t, page), lambda di, si, pt_ref: (di, 0, 0, 0, 0)),
                       pl.BlockSpec((1, h, t, LANES), seq4), pl.BlockSpec((1, h, t, LANES), seq4)],
            scratch_shapes=[pltpu.VMEM((n_pages, page, LANES), F32), pltpu.VMEM((n_pages // ppb, h, d), F32)],
        ),
        out_shape=[jax.ShapeDtypeStruct((db, n_pages, h, t, page), F32),
                   jax.ShapeDtypeStruct((db, h, t, LANES), F32),
                   jax.ShapeDtypeStruct((db, h, t, LANES), F32)],
        compiler_params=_params("parallel", "arbitrary"),
        name="moba_sample_probs",
    )(pt, q, k_new, *([cache_k] * group))

    return pl.pallas_call(
        functools.partial(_moba_sample_values_kernel, group=group),
        grid_spec=pltpu.PrefetchScalarGridSpec(
            num_scalar_prefetch=1,
            grid=(db, n_steps),
            in_specs=[pl.BlockSpec((1, group, h, t, page), lambda di, si, pt_ref: (di, si, 0, 0, 0)),
                      pl.BlockSpec((1, h, t, LANES), seq4), pl.BlockSpec((1, h, t, LANES), seq4),
                      pl.BlockSpec((1, h, t, d), seq4)] + cache_specs,
            out_specs=pl.BlockSpec((1, t, h * d), lambda di, si, pt_ref: (di, 0, 0)),
        ),
        out_shape=jax.ShapeDtypeStruct((db, t, h * d), F32),
        compiler_params=_params("parallel", "arbitrary"),
        name="moba_sample_values",
    )(pt, probs, p_own, inv, v_new, *([cache_v] * group))


def _gla_tables(c, key_rows):
    widths = []
    w = c
    while w >= 2:
        widths.append(w)
        w //= 2
    t = np.arange(c)
    prefix = [np.tril(np.ones((c, c), np.float32))]
    masks = [np.eye(c, dtype=np.float32)]
    for w in widths:
        base = (t // w) * w
        mid = base + w // 2
        upper = t >= mid
        pm = np.zeros((c, c), np.float32)
        for r in range(c):
            if upper[r]:
                pm[r, mid[r]:r + 1] = 1.0
            else:
                pm[r, r + 1:mid[r]] = 1.0
        prefix.append(pm)
        same = base[:, None] == base[None, :]
        masks.append((same & upper[:, None] & (~upper)[None, :]).astype(np.float32))
    pad = ((0, 0), (0, key_rows - c))
    return np.pad(np.concatenate(prefix, axis=0), pad), np.pad(np.stack(masks, axis=0), ((0, 0),) + pad)


def _pad_rows(x, n):
    if x.shape[0] == n:
        return x
    return jnp.concatenate([x, jnp.zeros((n - x.shape[0], x.shape[1]), x.dtype)], axis=0)


def _gla_chunk(q, k, v_f32, a, pre, msk_ref, state_t):
    n_levels, c, kr = msk_ref.shape
    v = v_f32.astype(BF16)
    a1 = a.astype(BF16)
    a2 = (a - a1.astype(F32)).astype(BF16)
    sums = jnp.dot(pre, a1, preferred_element_type=F32) + jnp.dot(pre, a2, preferred_element_type=F32)

    cum = sums[0:c]
    o = lax.dot_general((q * jnp.exp(cum)).astype(BF16), state_t.astype(BF16), NT_DIMS,
                        preferred_element_type=F32)
    scores = lax.dot_general(q.astype(BF16), _pad_rows(k, kr).astype(BF16), NT_DIMS,
                             preferred_element_type=F32) * msk_ref[0]
    for lv in range(1, n_levels):
        e = jnp.exp(sums[lv * c:(lv + 1) * c])
        s_lv = lax.dot_general((q * e).astype(BF16), _pad_rows(k * e, kr).astype(BF16), NT_DIMS,
                               preferred_element_type=F32)
        scores = scores + s_lv * msk_ref[lv]
    o = o + jnp.dot(scores.astype(BF16), v, preferred_element_type=F32)

    last = cum[c - 1:c]
    k_dec = _pad_rows(k * jnp.exp(last - cum), kr).astype(BF16)
    upd_t = jnp.dot(v_f32.T.astype(BF16), k_dec, preferred_element_type=F32)
    return o, jnp.exp(last) * state_t + upd_t


def _gla_kernel(*refs, has_state0, heads_per_step):
    if has_state0:
        q_ref, k_ref, v_ref, og_ref, a_ref, pre_ref, msk_ref, g_ref, s0_ref, y_ref, sout_ref, st_ref = refs
    else:
        q_ref, k_ref, v_ref, og_ref, a_ref, pre_ref, msk_ref, g_ref, y_ref, sout_ref, st_ref = refs
        s0_ref = None
    ci = pl.program_id(2)
    _, c, kr = msk_ref.shape
    n_rows = q_ref.shape[0]
    gk, gv = GLA_KEY_DIM, GLA_VAL_DIM

    @pl.when(ci == 0)
    def _():
        for hp in range(heads_per_step):
            if has_state0:
                st_ref[hp] = s0_ref[0, hp].T
            else:
                st_ref[hp] = jnp.zeros(st_ref.shape[1:], F32)

    pre = pre_ref[...]
    for hp in range(heads_per_step):
        kcols = slice(hp * gk, (hp + 1) * gk)
        vcols = slice(hp * gv, (hp + 1) * gv)
        q = _pad_rows(q_ref[:, kcols], c) * (GLA_KEY_DIM ** -0.5)
        k = _pad_rows(k_ref[:, kcols], c)
        v_f32 = _pad_rows(v_ref[:, vcols], kr)
        a = _pad_rows(a_ref[:, kcols], kr)
        o, new_state_t = _gla_chunk(q, k, v_f32, a, pre, msk_ref, st_ref[hp])
        st_ref[hp] = new_state_t

        @pl.when(ci == pl.num_programs(2) - 1)
        def _(hp=hp, new_state_t=new_state_t):
            sout_ref[0, hp] = new_state_t.T

        o = o[0:n_rows]
        ms = jnp.mean(o * o, axis=-1, keepdims=True)
        og = og_ref[:, vcols]
        y = (o * lax.rsqrt(ms + NORM_EPS) * g_ref[...]) * (og * _sigmoid(og))
        y_ref[:, vcols] = y.astype(y_ref.dtype)


GLA_HEADS_PER_STEP = 4


def _gla(u, log_a, gla_norm_g, state0, nseq, seqlen):
    m = u.shape[0]
    n_rows = min(GLA_CHUNK, seqlen)
    assert seqlen % n_rows == 0 and n_rows % 8 == 0
    n_chunks = seqlen // n_rows
    c = max(GLA_SHORT_CHUNK, n_rows)
    assert c & (c - 1) == 0 and c <= GLA_CHUNK
    pre_np, msk_np = _gla_tables(c, GLA_CHUNK)
    pre = jnp.asarray(pre_np, BF16)
    msk = jnp.asarray(msk_np, F32)
    gh, gk, gv = GLA_HEADS, GLA_KEY_DIM, GLA_VAL_DIM
    hps = GLA_HEADS_PER_STEP
    assert gh % hps == 0
    ng = gh // hps
    kw, vw = hps * gk, hps * gv
    k0, v0, og0 = (gh * gk) // kw, (2 * gh * gk) // vw, (2 * gh * gk + gh * gv) // vw
    assert k0 * kw == gh * gk and v0 * vw == 2 * gh * gk and og0 * vw == 2 * gh * gk + gh * gv
    row = lambda b, ci: b * n_chunks + ci
    in_specs = [
        pl.BlockSpec((n_rows, kw), lambda b, hg, ci: (row(b, ci), hg)),
        pl.BlockSpec((n_rows, kw), lambda b, hg, ci: (row(b, ci), k0 + hg)),
        pl.BlockSpec((n_rows, vw), lambda b, hg, ci: (row(b, ci), v0 + hg)),
        pl.BlockSpec((n_rows, vw), lambda b, hg, ci: (row(b, ci), og0 + hg)),
        pl.BlockSpec((n_rows, kw), lambda b, hg, ci: (row(b, ci), hg)),
        pl.BlockSpec(pre.shape, lambda b, hg, ci: (0, 0)),
        pl.BlockSpec(msk.shape, lambda b, hg, ci: (0, 0, 0)),
        pl.BlockSpec((1, gv), lambda b, hg, ci: (0, 0)),
    ]
    args = [u, u, u, u, log_a, pre, msk, gla_norm_g.reshape(1, gv)]
    if state0 is not None:
        in_specs.append(pl.BlockSpec((1, hps, gk, gv), lambda b, hg, ci: (b, hg, 0, 0)))
        args.append(state0)
    kern = functools.partial(_gla_kernel, has_state0=state0 is not None, heads_per_step=hps)
    return pl.pallas_call(
        kern,
        grid=(nseq, ng, n_chunks),
        in_specs=in_specs,
        out_specs=[pl.BlockSpec((n_rows, vw), lambda b, hg, ci: (row(b, ci), hg)),
                   pl.BlockSpec((1, hps, gk, gv), lambda b, hg, ci: (b, hg, 0, 0))],
        out_shape=[jax.ShapeDtypeStruct((m, gh * gv), BF16 if n_rows % 16 == 0 else F32),
                   jax.ShapeDtypeStruct((nseq, gh, gk, gv), F32)],
        scratch_shapes=[pltpu.VMEM((hps, gv, gk), F32)],
        compiler_params=_params("parallel", "parallel", "arbitrary"),
        name="gla",
    )(*args)


ROW_TILE = 512
WIDE_ROW_TILE = 1024
COL_TILE = 1024
MERGE_COL_TILE = 512


def _tile(m, pref):
    t = min(m, pref)
    assert m % t == 0
    return t


def _run_group(x, attend, state0, wts):
    nseq, seqlen, d = x.shape
    m = nseq * seqlen
    x2 = x.reshape(m, d)
    tm = _tile(m, ROW_TILE)
    tm_wide = _tile(m, WIDE_ROW_TILE)

    h = _rmsnorm(x2, wts["norm1_g"], tm)
    tmh = _tile(seqlen, WIDE_ROW_TILE) if seqlen >= 256 else m
    w_in, aw, nh = wts["w_in"], ATTN_HEADS * HEAD_DIM, ATTN_HEADS
    hpt = COL_TILE // HEAD_DIM
    q = _proj_heads(h, w_in, 0, nh, wts["q_norm_g"], nseq, seqlen, True, tmh, hpt, "proj_q")
    k = _proj_heads(h, w_in, aw, nh, wts["k_norm_g"], nseq, seqlen, True, tmh, hpt, "proj_k")
    v = _proj_heads(h, w_in, 2 * aw, nh, wts["k_norm_g"], nseq, seqlen, False, tmh, hpt, "proj_v")
    gla_cols = 2 * GLA_HEADS * (GLA_KEY_DIM + GLA_VAL_DIM)
    u = _matmul(h, w_in, 3 * aw, gla_cols, F32, tm_wide, COL_TILE, "proj_gla")
    log_a = _loga(h, wts["w_ag"], wts["w_a2"], wts["b_a"], tm)

    y_attn = attend(q, k, v)
    y_gla, gla_state = _gla(u, log_a, wts["gla_norm_g"], state0, nseq, seqlen)

    mixed = _merge(h, y_attn.reshape(m, -1), y_gla, wts["w_gates"],
                   wts["w_branch_attn"], wts["w_branch_gla"], tm, MERGE_COL_TILE)
    x1 = _matmul_residual(mixed, wts["w_out"], x2, tm_wide, COL_TILE, "out_proj")
    y = _mlp(x1, wts["norm2_g"], wts["w_up"], wts["w_down"], tm, COL_TILE)
    return y.reshape(nseq, seqlen, d), k, v, gla_state


def kernel(x_prompt, x_sample, cache_k, cache_v, state_gla, page_table, norm1_g, w_in, q_norm_g, k_norm_g,
           w_a2, b_a, gla_norm_g, w_branch_attn, w_branch_gla, w_out, norm2_g, w_up, w_down):
    depth = w_in.shape[0]
    d_model = x_prompt.shape[-1]
    ag0 = 3 * ATTN_HEADS * HEAD_DIM + 2 * GLA_HEADS * (GLA_KEY_DIM + GLA_VAL_DIM)

    yp, ys = x_prompt, x_sample
    outs = [[] for _ in range(6)]
    for l in range(depth):
        w_l = w_in[l].astype(BF16)
        assert w_l.shape[1] == ag0 + GLA_GATE_RANK + 2 * d_model
        wts = {
            "norm1_g": norm1_g[l], "q_norm_g": q_norm_g[l], "k_norm_g": k_norm_g[l],
            "w_in": w_l,
            "w_ag": jnp.pad(w_l[:, ag0:ag0 + GLA_GATE_RANK], ((0, 0), (0, LANES - GLA_GATE_RANK))),
            "w_gates": w_l[:, ag0 + GLA_GATE_RANK:],
            "w_a2": jnp.pad(w_a2[l], ((0, LANES - GLA_GATE_RANK), (0, 0))),
            "b_a": b_a[l].reshape(1, -1), "gla_norm_g": gla_norm_g[l],
            "w_branch_attn": w_branch_attn[l].astype(BF16), "w_branch_gla": w_branch_gla[l].astype(BF16),
            "w_out": w_out[l].astype(BF16), "norm2_g": norm2_g[l],
            "w_up": w_up[l].astype(BF16), "w_down": w_down[l].astype(BF16),
        }
        yp, k_l, v_l, s_l = _run_group(yp, _moba_prompt, None, wts)
        outs[0].append(k_l); outs[1].append(v_l); outs[2].append(s_l)
        attend_s = lambda q, k, v: _moba_sample(q, k, v, cache_k, cache_v, page_table, l)
        ys, k_l, v_l, s_l = _run_group(ys, attend_s, state_gla[l], wts)
        outs[3].append(k_l); outs[4].append(v_l); outs[5].append(s_l)
    stack = lambda xs: xs[0][None] if len(xs) == 1 else jnp.stack(xs)
    return (yp, ys) + tuple(stack(o) for o in outs)
```

```python
import functools

import numpy as np
import jax
import jax.numpy as jnp
from jax import lax
from jax.experimental import pallas as pl
from jax.experimental.pallas import tpu as pltpu

F32 = jnp.float32
BF16 = jnp.bfloat16
HIGHEST = lax.Precision.HIGHEST

NORM_EPS = 1e-6
ATTN_HEADS = 16
HEAD_DIM = 128
MOBA_BLOCK = 256
MOBA_TOPK = 3
GLA_HEADS = 4
GLA_KEY_DIM = 256
GLA_VAL_DIM = 512
GLA_GATE_RANK = 16
GLA_GATE_NORMALIZER = 16.0

LANES = 128
VMEM_LIMIT_BYTES = 56 * 1024 * 1024
GLA_CHUNK = 128
GLA_SHORT_CHUNK = 16
MASK_VALUE = -1e30
LOG2_E = 1.4426950408889634

NT_DIMS = (((1,), (1,)), ((), ()))
TN_DIMS = (((0,), (0,)), ((), ()))


def _params(*sem):
    return pltpu.CompilerParams(dimension_semantics=sem, vmem_limit_bytes=VMEM_LIMIT_BYTES)


def _sigmoid(x):
    return 1.0 / (1.0 + jnp.exp(-x))


def _rmsnorm_kernel(x_ref, g_ref, o_ref):
    x = x_ref[...]
    ms = jnp.mean(x * x, axis=-1, keepdims=True)
    o_ref[...] = (x * lax.rsqrt(ms + NORM_EPS) * g_ref[...]).astype(o_ref.dtype)


def _rmsnorm(x, g, tm):
    m, d = x.shape
    return pl.pallas_call(
        _rmsnorm_kernel,
        grid=(m // tm,),
        in_specs=[pl.BlockSpec((tm, d), lambda i: (i, 0)), pl.BlockSpec((1, d), lambda i: (0, 0))],
        out_specs=pl.BlockSpec((tm, d), lambda i: (i, 0)),
        out_shape=jax.ShapeDtypeStruct((m, d), BF16),
        compiler_params=_params("parallel"),
        name="rmsnorm",
    )(x, g.reshape(1, d))


def _matmul_kernel(a_ref, w_ref, o_ref, wb_ref):
    @pl.when(pl.program_id(1) == 0)
    def _():
        wb_ref[...] = w_ref[...].astype(BF16)

    o_ref[...] = jnp.dot(a_ref[...], wb_ref[...], preferred_element_type=F32).astype(o_ref.dtype)


def _matmul(a, w, col0, n, out_dtype, tm, tn, name):
    m, k = a.shape
    assert col0 % tn == 0 and n % tn == 0
    return pl.pallas_call(
        _matmul_kernel,
        grid=(n // tn, m // tm),
        in_specs=[pl.BlockSpec((tm, k), lambda j, i: (i, 0)),
                  pl.BlockSpec((k, tn), lambda j, i: (0, col0 // tn + j))],
        out_specs=pl.BlockSpec((tm, tn), lambda j, i: (i, j)),
        out_shape=jax.ShapeDtypeStruct((m, n), out_dtype),
        scratch_shapes=[pltpu.VMEM((k, tn), BF16)],
        compiler_params=_params("parallel", "arbitrary"),
        name=name,
    )(a, w)


def _matmul_residual_kernel(a_ref, w_ref, r_ref, o_ref):
    o_ref[...] = r_ref[...] + jnp.dot(a_ref[...], w_ref[...], preferred_element_type=F32)


def _matmul_residual(a, w, r, tm, tn, name):
    m, k = a.shape
    n = w.shape[1]
    return pl.pallas_call(
        _matmul_residual_kernel,
        grid=(m // tm, n // tn),
        in_specs=[pl.BlockSpec((tm, k), lambda i, j: (i, 0)), pl.BlockSpec((k, tn), lambda i, j: (0, j)),
                  pl.BlockSpec((tm, tn), lambda i, j: (i, j))],
        out_specs=pl.BlockSpec((tm, tn), lambda i, j: (i, j)),
        out_shape=jax.ShapeDtypeStruct((m, n), F32),
        compiler_params=_params("parallel", "parallel"),
        name=name,
    )(a, w, r)


def _proj_heads_kernel(h_ref, w_ref, g_ref, o_ref, wb_ref, *, normalize, heads_per_tile, seq_major):
    @pl.when((pl.program_id(1) == 0) & (pl.program_id(2) == 0))
    def _():
        wb_ref[...] = w_ref[...].astype(BF16)

    res = jnp.dot(h_ref[...], wb_ref[...], preferred_element_type=F32)
    for i in range(heads_per_tile):
        r = res[:, i * HEAD_DIM:(i + 1) * HEAD_DIM]
        if normalize:
            ms = jnp.mean(r * r, axis=-1, keepdims=True)
            r = r * lax.rsqrt(ms + NORM_EPS) * g_ref[...]
        if seq_major:
            o_ref[0, i] = r
        else:
            o_ref[:, i] = r.reshape(o_ref.shape[0], o_ref.shape[2], HEAD_DIM)


def _proj_heads(h, w, col0, nh, g, nseq, seqlen, normalize, tm, heads_per_tile, name):
    m, d = h.shape
    tn = heads_per_tile * HEAD_DIM
    assert col0 % tn == 0 and nh % heads_per_tile == 0
    ct0 = col0 // tn
    out_shape = jax.ShapeDtypeStruct((nseq, nh, seqlen, HEAD_DIM), F32)
    gain = g.reshape(1, HEAD_DIM)
    scratch = [pltpu.VMEM((d, tn), BF16)]
    sem = ("parallel", "arbitrary", "arbitrary")
    if seqlen % tm == 0:
        spt = seqlen // tm
        kern = functools.partial(_proj_heads_kernel, normalize=normalize, heads_per_tile=heads_per_tile, seq_major=True)
        return pl.pallas_call(
            kern,
            grid=(nh // heads_per_tile, nseq, spt),
            in_specs=[pl.BlockSpec((tm, d), lambda j, b, s: (b * spt + s, 0)),
                      pl.BlockSpec((d, tn), lambda j, b, s: (0, ct0 + j)),
                      pl.BlockSpec((1, HEAD_DIM), lambda j, b, s: (0, 0))],
            out_specs=pl.BlockSpec((1, heads_per_tile, tm, HEAD_DIM), lambda j, b, s: (b, j, s, 0)),
            out_shape=out_shape,
            scratch_shapes=scratch,
            compiler_params=_params(*sem),
            name=name,
        )(h, w, gain)
    assert tm == m
    kern = functools.partial(_proj_heads_kernel, normalize=normalize, heads_per_tile=heads_per_tile, seq_major=False)
    return pl.pallas_call(
        kern,
        grid=(nh // heads_per_tile, 1, 1),
        in_specs=[pl.BlockSpec((m, d), lambda j, b, s: (0, 0)),
                  pl.BlockSpec((d, tn), lambda j, b, s: (0, ct0 + j)),
                  pl.BlockSpec((1, HEAD_DIM), lambda j, b, s: (0, 0))],
        out_specs=pl.BlockSpec((nseq, heads_per_tile, seqlen, HEAD_DIM), lambda j, b, s: (0, j, 0, 0)),
        out_shape=out_shape,
        scratch_shapes=scratch,
        compiler_params=_params(*sem),
        name=name,
    )(h, w, gain)


def _loga_kernel(h_ref, wag_ref, wa2_ref, ba_ref, o_ref):
    ag = jnp.dot(h_ref[...], wag_ref[...], preferred_element_type=F32)
    z = jnp.dot(ag, wa2_ref[...], preferred_element_type=F32, precision=HIGHEST) + ba_ref[...]
    ls = jnp.minimum(z, 0.0) - jnp.log1p(jnp.exp(-jnp.abs(z)))
    o_ref[...] = ls * (1.0 / GLA_GATE_NORMALIZER)


def _loga(h, wag, wa2, ba, tm):
    m, d = h.shape
    n = wa2.shape[1]
    return pl.pallas_call(
        _loga_kernel,
        grid=(m // tm,),
        in_specs=[pl.BlockSpec((tm, d), lambda i: (i, 0)), pl.BlockSpec((d, LANES), lambda i: (0, 0)),
                  pl.BlockSpec((LANES, n), lambda i: (0, 0)), pl.BlockSpec((1, n), lambda i: (0, 0))],
        out_specs=pl.BlockSpec((tm, n), lambda i: (i, 0)),
        out_shape=jax.ShapeDtypeStruct((m, n), F32),
        compiler_params=_params("parallel"),
        name="gla_log_alpha",
    )(h, wag, wa2, ba)


def _merge_kernel(h_ref, ya_ref, yg_ref, wga_ref, wgg_ref, wba_ref, wbg_ref, o_ref):
    h = h_ref[...]
    ga = jnp.dot(h, wga_ref[...], preferred_element_type=F32)
    gg = jnp.dot(h, wgg_ref[...], preferred_element_type=F32)
    ba = jnp.dot(ya_ref[...].astype(BF16), wba_ref[...], preferred_element_type=F32)
    bg = jnp.dot(yg_ref[...].astype(BF16), wbg_ref[...], preferred_element_type=F32)
    o_ref[...] = (_sigmoid(ga) * ba + _sigmoid(gg) * bg).astype(o_ref.dtype)


def _merge(h, y_attn, y_gla, w_gates, wba, wbg, tm, tn):
    m, d = h.shape
    n = wba.shape[1]
    assert w_gates.shape[1] == 2 * n and n % tn == 0
    row = lambda i, j: (i, 0)
    col = lambda i, j: (0, j)
    return pl.pallas_call(
        _merge_kernel,
        grid=(m // tm, n // tn),
        in_specs=[pl.BlockSpec((tm, d), row), pl.BlockSpec((tm, y_attn.shape[1]), row),
                  pl.BlockSpec((tm, y_gla.shape[1]), row),
                  pl.BlockSpec((d, tn), col), pl.BlockSpec((d, tn), lambda i, j: (0, n // tn + j)),
                  pl.BlockSpec((wba.shape[0], tn), col), pl.BlockSpec((wbg.shape[0], tn), col)],
        out_specs=pl.BlockSpec((tm, tn), lambda i, j: (i, j)),
        out_shape=jax.ShapeDtypeStruct((m, n), BF16),
        compiler_params=_params("parallel", "parallel"),
        name="branch_merge",
    )(h, y_attn, y_gla, w_gates, w_gates, wba, wbg)


def _mlp_kernel(x_ref, g_ref, wup_ref, wdn_ref, o_ref, h_ref, acc_ref):
    f = pl.program_id(1)

    @pl.when(f == 0)
    def _():
        x = x_ref[...]
        ms = jnp.mean(x * x, axis=-1, keepdims=True)
        h_ref[...] = (x * lax.rsqrt(ms + NORM_EPS) * g_ref[...]).astype(BF16)
        acc_ref[...] = jnp.zeros_like(acc_ref)

    u = jnp.dot(h_ref[...], wup_ref[...], preferred_element_type=F32)
    a = jnp.square(jnp.maximum(u, 0.0)).astype(BF16)
    acc_ref[...] += jnp.dot(a, wdn_ref[...], preferred_element_type=F32)

    @pl.when(f == pl.num_programs(1) - 1)
    def _():
        o_ref[...] = x_ref[...] + acc_ref[...]


def _mlp(x, g, wup, wdn, tm, tf):
    m, d = x.shape
    ff = wup.shape[1]
    return pl.pallas_call(
        _mlp_kernel,
        grid=(m // tm, ff // tf),
        in_specs=[pl.BlockSpec((tm, d), lambda i, f: (i, 0)), pl.BlockSpec((1, d), lambda i, f: (0, 0)),
                  pl.BlockSpec((d, tf), lambda i, f: (0, f)), pl.BlockSpec((tf, d), lambda i, f: (f, 0))],
        out_specs=pl.BlockSpec((tm, d), lambda i, f: (i, 0)),
        out_shape=jax.ShapeDtypeStruct((m, d), F32),
        scratch_shapes=[pltpu.VMEM((tm, d), BF16), pltpu.VMEM((tm, d), F32)],
        compiler_params=_params("parallel", "arbitrary"),
        name="mlp",
    )(x, g.reshape(1, d), wup, wdn)


def _top_blocks(scores, valid, n_sel, axis):
    idxf = lax.broadcasted_iota(jnp.int32, scores.shape, axis).astype(F32)
    sv = jnp.where(valid, scores, -jnp.inf)
    sel = jnp.zeros(scores.shape, F32)
    for _ in range(n_sel):
        mx = jnp.max(sv, axis=axis, keepdims=True)
        idx = jnp.min(jnp.where(sv == mx, idxf, float(LANES)), axis=axis, keepdims=True)
        pick = idxf == idx
        sel = jnp.where(pick, 1.0, sel)
        sv = jnp.where(pick, -jnp.inf, sv)
    return jnp.where(valid, sel, 0.0)


def _moba_prompt_kernel(q_ref, k_ref, v_ref, onehot_ref, o_ref, *, n_blocks, heads_per_step):
    scale = HEAD_DIM ** -0.5
    blk = MOBA_BLOCK
    seq = q_ref.shape[2]
    row = lax.broadcasted_iota(jnp.int32, (blk, blk), 0)
    col = lax.broadcasted_iota(jnp.int32, (blk, blk), 1)
    eye = jnp.where(row == col, 1.0, 0.0).astype(BF16)

    heads = []
    for hp in range(heads_per_step):
        q = q_ref[0, hp]
        k = k_ref[0, hp]
        kmean = jnp.concatenate(
            [jnp.sum(k[j * blk:(j + 1) * blk], axis=0, keepdims=True) / blk for j in range(n_blocks)], axis=0)
        gate_t = lax.dot_general(kmean, q, NT_DIMS, precision=HIGHEST, preferred_element_type=F32)
        key_blk = lax.broadcasted_iota(jnp.int32, gate_t.shape, 0)
        qry_blk = lax.broadcasted_iota(jnp.int32, gate_t.shape, 1) // blk
        past = key_blk < qry_blk
        sel_t = _top_blocks(gate_t, past, min(MOBA_TOPK, n_blocks - 1), axis=0)
        allowed = (sel_t > 0.5) | (key_blk == qry_blk)
        mask_t = jnp.where(allowed, 0.0, MASK_VALUE)
        mask_t = jnp.concatenate([mask_t, jnp.zeros((LANES - n_blocks, seq), F32)], axis=0).astype(BF16)
        k_aug = jnp.concatenate([k.astype(BF16), onehot_ref[...]], axis=1)
        heads.append((q, mask_t, k_aug, v_ref[0, hp].astype(BF16)))

    for i in range(n_blocks):
        lo, hi = i * blk, (i + 1) * blk
        for hp, (q, mask_t, k_aug, v_b) in enumerate(heads):
            mask_i = lax.dot_general(eye, mask_t[:, lo:hi], NT_DIMS, preferred_element_type=F32)
            q_aug = jnp.concatenate([q[lo:hi].astype(BF16), mask_i.astype(BF16)], axis=1)
            s = lax.dot_general(q_aug, k_aug[0:hi], NT_DIMS, preferred_element_type=F32) * (scale * LOG2_E)
            s_own = jnp.where(col <= row, s[:, lo:hi], MASK_VALUE)
            m = jnp.max(s_own, axis=-1, keepdims=True)
            if i > 0:
                s_past = s[:, 0:lo]
                m = jnp.maximum(m, jnp.max(s_past, axis=-1, keepdims=True))
                p_past = jnp.exp2(s_past - m)
            p_own = jnp.exp2(s_own - m)
            l = jnp.sum(p_own, axis=-1, keepdims=True)
            o = jnp.dot(p_own.astype(BF16), v_b[lo:hi], preferred_element_type=F32)
            if i > 0:
                l = l + jnp.sum(p_past, axis=-1, keepdims=True)
                o = o + jnp.dot(p_past.astype(BF16), v_b[0:lo], preferred_element_type=F32)
            o_ref[0, lo:hi, hp * HEAD_DIM:(hp + 1) * HEAD_DIM] = (o / l).astype(o_ref.dtype)


MOBA_HEADS_PER_STEP = 1


def _moba_prompt(q, k, v):
    b, h, s, d = q.shape
    assert s % MOBA_BLOCK == 0 and d == HEAD_DIM
    nb = s // MOBA_BLOCK
    hps = MOBA_HEADS_PER_STEP
    assert nb % 8 == 0 and nb <= LANES and h % hps == 0
    onehot = jnp.asarray(np.arange(s)[:, None] // MOBA_BLOCK == np.arange(LANES)[None, :], BF16)
    head_map = lambda bi, hi: (bi, hi, 0, 0)
    return pl.pallas_call(
        functools.partial(_moba_prompt_kernel, n_blocks=nb, heads_per_step=hps),
        grid=(b, h // hps),
        in_specs=[pl.BlockSpec((1, hps, s, d), head_map), pl.BlockSpec((1, hps, s, d), head_map),
                  pl.BlockSpec((1, hps, s, d), head_map), pl.BlockSpec((s, LANES), lambda bi, hi: (0, 0))],
        out_specs=pl.BlockSpec((1, s, hps * d), lambda bi, hi: (bi, 0, hi)),
        out_shape=jax.ShapeDtypeStruct((b, s, h * d), BF16),
        compiler_params=_params("parallel", "parallel"),
        name="moba_prompt",
    )(q, k, v, onehot)


PAGE_GROUP = 16


def _own_head_lanes(x, n_heads, rows):
    lane_head = lax.broadcasted_iota(jnp.int32, (rows, LANES), 1) // (LANES // n_heads)
    out = x[0:rows]
    for h in range(1, n_heads):
        out = jnp.where(lane_head == h, x[h * rows:(h + 1) * rows], out)
    return out


def _moba_sample_probs_kernel(pt_ref, q_ref, kn_ref, *rest, n_pages, pages_per_block, group):
    kc_refs = rest[:group]
    p_ref, pown_ref, inv_ref, lt_ref, bsum_ref = rest[group:]
    step = pl.program_id(1)
    n_heads, n_tok, d = q_ref.shape[1:]
    page = kc_refs[0].shape[3]
    n_blocks = n_pages // pages_per_block
    scale = HEAD_DIM ** -0.5
    q_all = q_ref[0].reshape(n_heads * n_tok, d)
    q_all_b = q_all.astype(BF16)

    page_sums = []
    for g in range(group):
        keys = kc_refs[g][0, 0]
        page_sums.append(jnp.sum(keys, axis=1))
        full = lax.dot_general(keys.reshape(n_heads * page, d).astype(BF16), q_all_b, NT_DIMS,
                               preferred_element_type=F32)
        lt_ref[step * group + g] = _own_head_lanes(full, n_heads, page) * scale
    for bb in range(group // pages_per_block):
        tot = page_sums[bb * pages_per_block]
        for pp in range(1, pages_per_block):
            tot = tot + page_sums[bb * pages_per_block + pp]
        bsum_ref[step * (group // pages_per_block) + bb] = tot

    @pl.when(step == pl.num_programs(1) - 1)
    def _():
        lane = lax.broadcasted_iota(jnp.int32, (LANES, LANES), 1)
        row = lax.broadcasted_iota(jnp.int32, (LANES, LANES), 0)
        kmean = [jnp.concatenate([bsum_ref[b][h:h + 1] for b in range(n_blocks)], axis=0) / MOBA_BLOCK
                 for h in range(n_heads)]
        gate_full = lax.dot_general(jnp.concatenate(kmean, axis=0), q_all, NT_DIMS, precision=HIGHEST,
                                    preferred_element_type=F32)
        gate_t = _own_head_lanes(gate_full, n_heads, n_blocks)
        sel_t = _top_blocks(gate_t, row[0:n_blocks] >= 0, min(MOBA_TOPK, n_blocks), axis=0)
        kn_all = kn_ref[0].reshape(n_heads * n_tok, d).astype(BF16)
        own_full = lax.dot_general(kn_all, q_all_b, NT_DIMS, preferred_element_type=F32)
        own_t = _own_head_lanes(own_full, n_heads, n_tok) * scale
        own_t = jnp.where(row[0:n_tok] <= lane[0:n_tok] % n_tok, own_t, MASK_VALUE)

        picked = [sel_t[b:b + 1] > 0.5 for b in range(n_blocks)]
        mrun = jnp.full((page, LANES), MASK_VALUE, F32)
        for pg in range(n_pages):
            mrun = jnp.maximum(mrun, jnp.where(picked[pg // pages_per_block], lt_ref[pg], MASK_VALUE))
        m = jnp.maximum(jnp.max(mrun, axis=0, keepdims=True), jnp.max(own_t, axis=0, keepdims=True))
        e_own = jnp.exp(own_t - m)
        lrun = jnp.zeros((page, LANES), F32)
        for pg in range(n_pages):
            e = jnp.where(picked[pg // pages_per_block], jnp.exp(lt_ref[pg] - m), 0.0)
            lrun = lrun + e
            p_ref[0, pg] = e.T.reshape(n_heads, n_tok, page)
        total = jnp.sum(lrun, axis=0, keepdims=True) + jnp.sum(e_own, axis=0, keepdims=True)
        e_own_pad = jnp.concatenate([e_own, jnp.zeros((LANES - n_tok, LANES), F32)], axis=0)
        pown_ref[0] = e_own_pad.T.reshape(n_heads, n_tok, LANES)
        inv_ref[0] = jnp.broadcast_to(1.0 / total, (LANES, LANES)).T.reshape(n_heads, n_tok, LANES)


def _moba_sample_values_kernel(pt_ref, p_ref, pown_ref, inv_ref, vn_ref, *rest, group):
    vc_refs = rest[:group]
    o_ref = rest[group]
    step = pl.program_id(1)
    n_heads = vn_ref.shape[1]
    n_tok = vn_ref.shape[2]
    head = lambda h: slice(h * HEAD_DIM, (h + 1) * HEAD_DIM)

    @pl.when(step == 0)
    def _():
        for h in range(n_heads):
            pown = pown_ref[0, h]
            vn = vn_ref[0, h]
            own = pown[:, 0:1] * vn[0:1, :]
            for t in range(1, n_tok):
                own = own + pown[:, t:t + 1] * vn[t:t + 1, :]
            o_ref[0, :, head(h)] = own

    for h in range(n_heads):
        w = jnp.concatenate([p_ref[0, g, h] for g in range(group)], axis=1).astype(BF16)
        vals = jnp.concatenate([vc_refs[g][0, 0, h] for g in range(group)], axis=0).astype(BF16)
        o_ref[0, :, head(h)] += jnp.dot(w, vals, preferred_element_type=F32)

    @pl.when(step == pl.num_programs(1) - 1)
    def _():
        for h in range(n_heads):
            o_ref[0, :, head(h)] = o_ref[0, :, head(h)] * inv_ref[0, h]


def _moba_sample(q, k_new, v_new, cache_k, cache_v, page_table, layer):
    db, h, t, d = q.shape
    n_pages = page_table.shape[1]
    page = cache_k.shape[3]
    ppb = MOBA_BLOCK // page
    group = PAGE_GROUP
    assert page == LANES and d == HEAD_DIM and MOBA_BLOCK % page == 0
    assert (n_pages * page) % MOBA_BLOCK == 0 and t <= MOBA_BLOCK
    assert n_pages // ppb <= LANES and n_pages // ppb >= MOBA_TOPK
    assert n_pages % group == 0 and group % ppb == 0 and h * t == LANES and (n_pages // ppb) % 8 == 0
    n_steps = n_pages // group
    pt = page_table.reshape(-1).astype(jnp.int32)

    seq4 = lambda di, si, pt_ref: (di, 0, 0, 0)
    cache_specs = [
        pl.BlockSpec((1, 1, h, page, d),
                     lambda di, si, pt_ref, g=g: (layer, pt_ref[di * n_pages + si * group + g], 0, 0, 0))
        for g in range(group)]

    probs, p_own, inv = pl.pallas_call(
        functools.partial(_moba_sample_probs_kernel, n_pages=n_pages, pages_per_block=ppb, group=group),
        grid_spec=pltpu.PrefetchScalarGridSpec(
            num_scalar_prefetch=1,
            grid=(db, n_steps),
            in_specs=[pl.BlockSpec((1, h, t, d), seq4), pl.BlockSpec((1, h, t, d), seq4)] + cache_specs,
            out_specs=[pl.BlockSpec((1, n_pages, h, t, page), lambda di, si, pt_ref: (di, 0, 0, 0, 0)),
                       pl.BlockSpec((1, h, t, LANES), seq4), pl.BlockSpec((1, h, t, LANES), seq4)],
            scratch_shapes=[pltpu.VMEM((n_pages, page, LANES), F32), pltpu.VMEM((n_pages // ppb, h, d), F32)],
        ),
        out_shape=[jax.ShapeDtypeStruct((db, n_pages, h, t, page), F32),
                   jax.ShapeDtypeStruct((db, h, t, LANES), F32),
                   jax.ShapeDtypeStruct((db, h, t, LANES), F32)],
        compiler_params=_params("parallel", "arbitrary"),
        name="moba_sample_probs",
    )(pt, q, k_new, *([cache_k] * group))

    return pl.pallas_call(
        functools.partial(_moba_sample_values_kernel, group=group),
        grid_spec=pltpu.PrefetchScalarGridSpec(
            num_scalar_prefetch=1,
            grid=(db, n_steps),
            in_specs=[pl.BlockSpec((1, group, h, t, page), lambda di, si, pt_ref: (di, si, 0, 0, 0)),
                      pl.BlockSpec((1, h, t, LANES), seq4), pl.BlockSpec((1, h, t, LANES), seq4),
                      pl.BlockSpec((1, h, t, d), seq4)] + cache_specs,
            out_specs=pl.BlockSpec((1, t, h * d), lambda di, si, pt_ref: (di, 0, 0)),
        ),
        out_shape=jax.ShapeDtypeStruct((db, t, h * d), F32),
        compiler_params=_params("parallel", "arbitrary"),
        name="moba_sample_values",
    )(pt, probs, p_own, inv, v_new, *([cache_v] * group))


def _gla_tables(c, key_rows):
    widths = []
    w = c
    while w >= 2:
        widths.append(w)
        w //= 2
    t = np.arange(c)
    prefix = [np.tril(np.ones((c, c), np.float32))]
    masks = [np.eye(c, dtype=np.float32)]
    for w in widths:
        base = (t // w) * w
        mid = base + w // 2
        upper = t >= mid
        pm = np.zeros((c, c), np.float32)
        for r in range(c):
            if upper[r]:
                pm[r, mid[r]:r + 1] = 1.0
            else:
                pm[r, r + 1:mid[r]] = 1.0
        prefix.append(pm)
        same = base[:, None] == base[None, :]
        masks.append((same & upper[:, None] & (~upper)[None, :]).astype(np.float32))
    pad = ((0, 0), (0, key_rows - c))
    return np.pad(np.concatenate(prefix, axis=0), pad), np.pad(np.stack(masks, axis=0), ((0, 0),) + pad)


def _pad_rows(x, n):
    if x.shape[0] == n:
        return x
    return jnp.concatenate([x, jnp.zeros((n - x.shape[0], x.shape[1]), x.dtype)], axis=0)


def _gla_chunk(q, k, v_f32, a, pre, msk_ref, state_t):
    n_levels, c, kr = msk_ref.shape
    v = v_f32.astype(BF16)
    a1 = a.astype(BF16)
    a2 = (a - a1.astype(F32)).astype(BF16)
    sums = jnp.dot(pre, a1, preferred_element_type=F32) + jnp.dot(pre, a2, preferred_element_type=F32)

    cum = sums[0:c]
    o = lax.dot_general((q * jnp.exp(cum)).astype(BF16), state_t.astype(BF16), NT_DIMS,
                        preferred_element_type=F32)
    scores = lax.dot_general(q.astype(BF16), _pad_rows(k, kr).astype(BF16), NT_DIMS,
                             preferred_element_type=F32) * msk_ref[0]
    for lv in range(1, n_levels):
        e = jnp.exp(sums[lv * c:(lv + 1) * c])
        s_lv = lax.dot_general((q * e).astype(BF16), _pad_rows(k * e, kr).astype(BF16), NT_DIMS,
                               preferred_element_type=F32)
        scores = scores + s_lv * msk_ref[lv]
    o = o + jnp.dot(scores.astype(BF16), v, preferred_element_type=F32)

    last = cum[c - 1:c]
    k_dec = _pad_rows(k * jnp.exp(last - cum), kr).astype(BF16)
    upd_t = jnp.dot(v_f32.T.astype(BF16), k_dec, preferred_element_type=F32)
    return o, jnp.exp(last) * state_t + upd_t


def _gla_kernel(*refs, has_state0, heads_per_step):
    if has_state0:
        q_ref, k_ref, v_ref, og_ref, a_ref, pre_ref, msk_ref, g_ref, s0_ref, y_ref, sout_ref, st_ref = refs
    else:
        q_ref, k_ref, v_ref, og_ref, a_ref, pre_ref, msk_ref, g_ref, y_ref, sout_ref, st_ref = refs
        s0_ref = None
    ci = pl.program_id(2)
    _, c, kr = msk_ref.shape
    n_rows = q_ref.shape[0]
    gk, gv = GLA_KEY_DIM, GLA_VAL_DIM

    @pl.when(ci == 0)
    def _():
        for hp in range(heads_per_step):
            if has_state0:
                st_ref[hp] = s0_ref[0, hp].T
            else:
                st_ref[hp] = jnp.zeros(st_ref.shape[1:], F32)

    pre = pre_ref[...]
    for hp in range(heads_per_step):
        kcols = slice(hp * gk, (hp + 1) * gk)
        vcols = slice(hp * gv, (hp + 1) * gv)
        q = _pad_rows(q_ref[:, kcols], c) * (GLA_KEY_DIM ** -0.5)
        k = _pad_rows(k_ref[:, kcols], c)
        v_f32 = _pad_rows(v_ref[:, vcols], kr)
        a = _pad_rows(a_ref[:, kcols], kr)
        o, new_state_t = _gla_chunk(q, k, v_f32, a, pre, msk_ref, st_ref[hp])
        st_ref[hp] = new_state_t

        @pl.when(ci == pl.num_programs(2) - 1)
        def _(hp=hp, new_state_t=new_state_t):
            sout_ref[0, hp] = new_state_t.T

        o = o[0:n_rows]
        ms = jnp.mean(o * o, axis=-1, keepdims=True)
        og = og_ref[:, vcols]
        y = (o * lax.rsqrt(ms + NORM_EPS) * g_ref[...]) * (og * _sigmoid(og))
        y_ref[:, vcols] = y.astype(y_ref.dtype)


GLA_HEADS_PER_STEP = 4


def _gla(u, log_a, gla_norm_g, state0, nseq, seqlen):
    m = u.shape[0]
    n_rows = min(GLA_CHUNK, seqlen)
    assert seqlen % n_rows == 0 and n_rows % 8 == 0
    n_chunks = seqlen // n_rows
    c = max(GLA_SHORT_CHUNK, n_rows)
    assert c & (c - 1) == 0 and c <= GLA_CHUNK
    pre_np, msk_np = _gla_tables(c, GLA_CHUNK)
    pre = jnp.asarray(pre_np, BF16)
    msk = jnp.asarray(msk_np, F32)
    gh, gk, gv = GLA_HEADS, GLA_KEY_DIM, GLA_VAL_DIM
    hps = GLA_HEADS_PER_STEP
    assert gh % hps == 0
    ng = gh // hps
    kw, vw = hps * gk, hps * gv
    k0, v0, og0 = (gh * gk) // kw, (2 * gh * gk) // vw, (2 * gh * gk + gh * gv) // vw
    assert k0 * kw == gh * gk and v0 * vw == 2 * gh * gk and og0 * vw == 2 * gh * gk + gh * gv
    row = lambda b, ci: b * n_chunks + ci
    in_specs = [
        pl.BlockSpec((n_rows, kw), lambda b, hg, ci: (row(b, ci), hg)),
        pl.BlockSpec((n_rows, kw), lambda b, hg, ci: (row(b, ci), k0 + hg)),
        pl.BlockSpec((n_rows, vw), lambda b, hg, ci: (row(b, ci), v0 + hg)),
        pl.BlockSpec((n_rows, vw), lambda b, hg, ci: (row(b, ci), og0 + hg)),
        pl.BlockSpec((n_rows, kw), lambda b, hg, ci: (row(b, ci), hg)),
        pl.BlockSpec(pre.shape, lambda b, hg, ci: (0, 0)),
        pl.BlockSpec(msk.shape, lambda b, hg, ci: (0, 0, 0)),
        pl.BlockSpec((1, gv), lambda b, hg, ci: (0, 0)),
    ]
    args = [u, u, u, u, log_a, pre, msk, gla_norm_g.reshape(1, gv)]
    if state0 is not None:
        in_specs.append(pl.BlockSpec((1, hps, gk, gv), lambda b, hg, ci: (b, hg, 0, 0)))
        args.append(state0)
    kern = functools.partial(_gla_kernel, has_state0=state0 is not None, heads_per_step=hps)
    return pl.pallas_call(
        kern,
        grid=(nseq, ng, n_chunks),
        in_specs=in_specs,
        out_specs=[pl.BlockSpec((n_rows, vw), lambda b, hg, ci: (row(b, ci), hg)),
                   pl.BlockSpec((1, hps, gk, gv), lambda b, hg, ci: (b, hg, 0, 0))],
        out_shape=[jax.ShapeDtypeStruct((m, gh * gv), BF16 if n_rows % 16 == 0 else F32),
                   jax.ShapeDtypeStruct((nseq, gh, gk, gv), F32)],
        scratch_shapes=[pltpu.VMEM((hps, gv, gk), F32)],
        compiler_params=_params("parallel", "parallel", "arbitrary"),
        name="gla",
    )(*args)


ROW_TILE = 512
WIDE_ROW_TILE = 1024
COL_TILE = 1024
MERGE_COL_TILE = 512


def _tile(m, pref):
    t = min(m, pref)
    assert m % t == 0
    return t


def _run_group(x, attend, state0, wts):
    nseq, seqlen, d = x.shape
    m = nseq * seqlen
    x2 = x.reshape(m, d)
    tm = _tile(m, ROW_TILE)
    tm_wide = _tile(m, WIDE_ROW_TILE)

    h = _rmsnorm(x2, wts["norm1_g"], tm)
    tmh = _tile(seqlen, WIDE_ROW_TILE) if seqlen >= 256 else m
    w_in, aw, nh = wts["w_in"], ATTN_HEADS * HEAD_DIM, ATTN_HEADS
    hpt = COL_TILE // HEAD_DIM
    q = _proj_heads(h, w_in, 0, nh, wts["q_norm_g"], nseq, seqlen, True, tmh, hpt, "proj_q")
    k = _proj_heads(h, w_in, aw, nh, wts["k_norm_g"], nseq, seqlen, True, tmh, hpt, "proj_k")
    v = _proj_heads(h, w_in, 2 * aw, nh, wts["k_norm_g"], nseq, seqlen, False, tmh, hpt, "proj_v")
    gla_cols = 2 * GLA_HEADS * (GLA_KEY_DIM + GLA_VAL_DIM)
    u = _matmul(h, w_in, 3 * aw, gla_cols, F32, tm_wide, COL_TILE, "proj_gla")
    log_a = _loga(h, wts["w_ag"], wts["w_a2"], wts["b_a"], tm)

    y_attn = attend(q, k, v)
    y_gla, gla_state = _gla(u, log_a, wts["gla_norm_g"], state0, nseq, seqlen)

    mixed = _merge(h, y_attn.reshape(m, -1), y_gla, wts["w_gates"],
                   wts["w_branch_attn"], wts["w_branch_gla"], tm, MERGE_COL_TILE)
    x1 = _matmul_residual(mixed, wts["w_out"], x2, tm_wide, COL_TILE, "out_proj")
    y = _mlp(x1, wts["norm2_g"], wts["w_up"], wts["w_down"], tm, COL_TILE)
    return y.reshape(nseq, seqlen, d), k, v, gla_state


def kernel(x_prompt, x_sample, cache_k, cache_v, state_gla, page_table, norm1_g, w_in, q_norm_g, k_norm_g,
           w_a2, b_a, gla_norm_g, w_branch_attn, w_branch_gla, w_out, norm2_g, w_up, w_down):
    depth = w_in.shape[0]
    d_model = x_prompt.shape[-1]
    ag0 = 3 * ATTN_HEADS * HEAD_DIM + 2 * GLA_HEADS * (GLA_KEY_DIM + GLA_VAL_DIM)

    yp, ys = x_prompt, x_sample
    outs = [[] for _ in range(6)]
    for l in range(depth):
        w_l = w_in[l]
        assert w_l.shape[1] == ag0 + GLA_GATE_RANK + 2 * d_model
        wts = {
            "norm1_g": norm1_g[l], "q_norm_g": q_norm_g[l], "k_norm_g": k_norm_g[l],
            "w_in": w_l,
            "w_ag": jnp.pad(w_l[:, ag0:ag0 + GLA_GATE_RANK].astype(BF16), ((0, 0), (0, LANES - GLA_GATE_RANK))),
            "w_gates": w_l[:, ag0 + GLA_GATE_RANK:].astype(BF16),
            "w_a2": jnp.pad(w_a2[l], ((0, LANES - GLA_GATE_RANK), (0, 0))),
            "b_a": b_a[l].reshape(1, -1), "gla_norm_g": gla_norm_g[l],
            "w_branch_attn": w_branch_attn[l].astype(BF16), "w_branch_gla": w_branch_gla[l].astype(BF16),
            "w_out": w_out[l].astype(BF16), "norm2_g": norm2_g[l],
            "w_up": w_up[l].astype(BF16), "w_down": w_down[l].astype(BF16),
        }
        yp, k_l, v_l, s_l = _run_group(yp, _moba_prompt, None, wts)
        outs[0].append(k_l); outs[1].append(v_l); outs[2].append(s_l)
        attend_s = lambda q, k, v: _moba_sample(q, k, v, cache_k, cache_v, page_table, l)
        ys, k_l, v_l, s_l = _run_group(ys, attend_s, state_gla[l], wts)
        outs[3].append(k_l); outs[4].append(v_l); outs[5].append(s_l)
    stack = lambda xs: xs[0][None] if len(xs) == 1 else jnp.stack(xs)
    return (yp, ys) + tuple(stack(o) for o in outs)
```

```python
import functools

import numpy as np
import jax
import jax.numpy as jnp
from jax import lax
from jax.experimental import pallas as pl
from jax.experimental.pallas import tpu as pltpu

F32 = jnp.float32
BF16 = jnp.bfloat16
HIGHEST = lax.Precision.HIGHEST

NORM_EPS = 1e-6
ATTN_HEADS = 16
HEAD_DIM = 128
MOBA_BLOCK = 256
MOBA_TOPK = 3
GLA_HEADS = 4
GLA_KEY_DIM = 256
GLA_VAL_DIM = 512
GLA_GATE_RANK = 16
GLA_GATE_NORMALIZER = 16.0

LANES = 128
VMEM_LIMIT_BYTES = 56 * 1024 * 1024
GLA_CHUNK = 128
GLA_SHORT_CHUNK = 16
MASK_VALUE = -1e30
LOG2_E = 1.4426950408889634

NT_DIMS = (((1,), (1,)), ((), ()))
TN_DIMS = (((0,), (0,)), ((), ()))


def _params(*sem):
    return pltpu.CompilerParams(dimension_semantics=sem, vmem_limit_bytes=VMEM_LIMIT_BYTES)


def _sigmoid(x):
    return 1.0 / (1.0 + jnp.exp(-x))


def _rmsnorm_kernel(x_ref, g_ref, o_ref):
    x = x_ref[...]
    ms = jnp.mean(x * x, axis=-1, keepdims=True)
    o_ref[...] = (x * lax.rsqrt(ms + NORM_EPS) * g_ref[...]).astype(o_ref.dtype)


def _rmsnorm(x, g, tm):
    m, d = x.shape
    return pl.pallas_call(
        _rmsnorm_kernel,
        grid=(m // tm,),
        in_specs=[pl.BlockSpec((tm, d), lambda i: (i, 0)), pl.BlockSpec((1, d), lambda i: (0, 0))],
        out_specs=pl.BlockSpec((tm, d), lambda i: (i, 0)),
        out_shape=jax.ShapeDtypeStruct((m, d), BF16),
        compiler_params=_params("parallel"),
        name="rmsnorm",
    )(x, g.reshape(1, d))


def _matmul_kernel(a_ref, w_ref, o_ref):
    o_ref[...] = jnp.dot(a_ref[...], w_ref[...], preferred_element_type=F32).astype(o_ref.dtype)


def _matmul(a, w, col0, n, out_dtype, tm, tn, name):
    m, k = a.shape
    assert col0 % tn == 0 and n % tn == 0
    return pl.pallas_call(
        _matmul_kernel,
        grid=(m // tm, n // tn),
        in_specs=[pl.BlockSpec((tm, k), lambda i, j: (i, 0)),
                  pl.BlockSpec((k, tn), lambda i, j: (0, col0 // tn + j))],
        out_specs=pl.BlockSpec((tm, tn), lambda i, j: (i, j)),
        out_shape=jax.ShapeDtypeStruct((m, n), out_dtype),
        compiler_params=_params("parallel", "parallel"),
        name=name,
    )(a, w)


def _matmul_residual_kernel(a_ref, w_ref, r_ref, o_ref):
    o_ref[...] = r_ref[...] + jnp.dot(a_ref[...], w_ref[...], preferred_element_type=F32)


def _matmul_residual(a, w, r, tm, tn, name):
    m, k = a.shape
    n = w.shape[1]
    return pl.pallas_call(
        _matmul_residual_kernel,
        grid=(m // tm, n // tn),
        in_specs=[pl.BlockSpec((tm, k), lambda i, j: (i, 0)), pl.BlockSpec((k, tn), lambda i, j: (0, j)),
                  pl.BlockSpec((tm, tn), lambda i, j: (i, j))],
        out_specs=pl.BlockSpec((tm, tn), lambda i, j: (i, j)),
        out_shape=jax.ShapeDtypeStruct((m, n), F32),
        compiler_params=_params("parallel", "parallel"),
        name=name,
    )(a, w, r)


def _proj_heads_kernel(h_ref, w_ref, g_ref, o_ref, *, normalize, heads_per_tile, seq_major):
    res = jnp.dot(h_ref[...], w_ref[...], preferred_element_type=F32)
    for i in range(heads_per_tile):
        r = res[:, i * HEAD_DIM:(i + 1) * HEAD_DIM]
        if normalize:
            ms = jnp.mean(r * r, axis=-1, keepdims=True)
            r = r * lax.rsqrt(ms + NORM_EPS) * g_ref[...]
        if seq_major:
            o_ref[0, i] = r
        else:
            o_ref[:, i] = r.reshape(o_ref.shape[0], o_ref.shape[2], HEAD_DIM)


def _proj_heads(h, w, col0, nh, g, nseq, seqlen, normalize, tm, heads_per_tile, name):
    m, d = h.shape
    tn = heads_per_tile * HEAD_DIM
    assert col0 % tn == 0 and nh % heads_per_tile == 0
    ct0 = col0 // tn
    out_shape = jax.ShapeDtypeStruct((nseq, nh, seqlen, HEAD_DIM), F32)
    gain = g.reshape(1, HEAD_DIM)
    if seqlen % tm == 0:
        spt = seqlen // tm
        kern = functools.partial(_proj_heads_kernel, normalize=normalize, heads_per_tile=heads_per_tile, seq_major=True)
        return pl.pallas_call(
            kern,
            grid=(nseq, spt, nh // heads_per_tile),
            in_specs=[pl.BlockSpec((tm, d), lambda b, s, j: (b * spt + s, 0)),
                      pl.BlockSpec((d, tn), lambda b, s, j: (0, ct0 + j)),
                      pl.BlockSpec((1, HEAD_DIM), lambda b, s, j: (0, 0))],
            out_specs=pl.BlockSpec((1, heads_per_tile, tm, HEAD_DIM), lambda b, s, j: (b, j, s, 0)),
            out_shape=out_shape,
            compiler_params=_params("parallel", "parallel", "parallel"),
            name=name,
        )(h, w, gain)
    assert tm == m
    kern = functools.partial(_proj_heads_kernel, normalize=normalize, heads_per_tile=heads_per_tile, seq_major=False)
    return pl.pallas_call(
        kern,
        grid=(nh // heads_per_tile,),
        in_specs=[pl.BlockSpec((m, d), lambda j: (0, 0)),
                  pl.BlockSpec((d, tn), lambda j: (0, ct0 + j)),
                  pl.BlockSpec((1, HEAD_DIM), lambda j: (0, 0))],
        out_specs=pl.BlockSpec((nseq, heads_per_tile, seqlen, HEAD_DIM), lambda j: (0, j, 0, 0)),
        out_shape=out_shape,
        compiler_params=_params("parallel"),
        name=name,
    )(h, w, gain)


def _loga_kernel(h_ref, wag_ref, wa2_ref, ba_ref, o_ref):
    ag = jnp.dot(h_ref[...], wag_ref[...], preferred_element_type=F32)
    z = jnp.dot(ag, wa2_ref[...], preferred_element_type=F32, precision=HIGHEST) + ba_ref[...]
    ls = jnp.minimum(z, 0.0) - jnp.log1p(jnp.exp(-jnp.abs(z)))
    o_ref[...] = ls * (1.0 / GLA_GATE_NORMALIZER)


def _loga(h, wag, wa2, ba, tm):
    m, d = h.shape
    n = wa2.shape[1]
    return pl.pallas_call(
        _loga_kernel,
        grid=(m // tm,),
        in_specs=[pl.BlockSpec((tm, d), lambda i: (i, 0)), pl.BlockSpec((d, LANES), lambda i: (0, 0)),
                  pl.BlockSpec((LANES, n), lambda i: (0, 0)), pl.BlockSpec((1, n), lambda i: (0, 0))],
        out_specs=pl.BlockSpec((tm, n), lambda i: (i, 0)),
        out_shape=jax.ShapeDtypeStruct((m, n), F32),
        compiler_params=_params("parallel"),
        name="gla_log_alpha",
    )(h, wag, wa2, ba)


def _merge_kernel(h_ref, ya_ref, yg_ref, wga_ref, wgg_ref, wba_ref, wbg_ref, o_ref):
    h = h_ref[...]
    ga = jnp.dot(h, wga_ref[...], preferred_element_type=F32)
    gg = jnp.dot(h, wgg_ref[...], preferred_element_type=F32)
    ba = jnp.dot(ya_ref[...].astype(BF16), wba_ref[...], preferred_element_type=F32)
    bg = jnp.dot(yg_ref[...].astype(BF16), wbg_ref[...], preferred_element_type=F32)
    o_ref[...] = (_sigmoid(ga) * ba + _sigmoid(gg) * bg).astype(o_ref.dtype)


def _merge(h, y_attn, y_gla, w_gates, wba, wbg, tm, tn):
    m, d = h.shape
    n = wba.shape[1]
    assert w_gates.shape[1] == 2 * n and n % tn == 0
    row = lambda i, j: (i, 0)
    col = lambda i, j: (0, j)
    return pl.pallas_call(
        _merge_kernel,
        grid=(m // tm, n // tn),
        in_specs=[pl.BlockSpec((tm, d), row), pl.BlockSpec((tm, y_attn.shape[1]), row),
                  pl.BlockSpec((tm, y_gla.shape[1]), row),
                  pl.BlockSpec((d, tn), col), pl.BlockSpec((d, tn), lambda i, j: (0, n // tn + j)),
                  pl.BlockSpec((wba.shape[0], tn), col), pl.BlockSpec((wbg.shape[0], tn), col)],
        out_specs=pl.BlockSpec((tm, tn), lambda i, j: (i, j)),
        out_shape=jax.ShapeDtypeStruct((m, n), BF16),
        compiler_params=_params("parallel", "parallel"),
        name="branch_merge",
    )(h, y_attn, y_gla, w_gates, w_gates, wba, wbg)


def _mlp_kernel(x_ref, g_ref, wup_ref, wdn_ref, o_ref, h_ref, acc_ref):
    f = pl.program_id(1)

    @pl.when(f == 0)
    def _():
        x = x_ref[...]
        ms = jnp.mean(x * x, axis=-1, keepdims=True)
        h_ref[...] = (x * lax.rsqrt(ms + NORM_EPS) * g_ref[...]).astype(BF16)
        acc_ref[...] = jnp.zeros_like(acc_ref)

    u = jnp.dot(h_ref[...], wup_ref[...], preferred_element_type=F32)
    a = jnp.square(jnp.maximum(u, 0.0)).astype(BF16)
    acc_ref[...] += jnp.dot(a, wdn_ref[...], preferred_element_type=F32)

    @pl.when(f == pl.num_programs(1) - 1)
    def _():
        o_ref[...] = x_ref[...] + acc_ref[...]


def _mlp(x, g, wup, wdn, tm, tf):
    m, d = x.shape
    ff = wup.shape[1]
    return pl.pallas_call(
        _mlp_kernel,
        grid=(m // tm, ff // tf),
        in_specs=[pl.BlockSpec((tm, d), lambda i, f: (i, 0)), pl.BlockSpec((1, d), lambda i, f: (0, 0)),
                  pl.BlockSpec((d, tf), lambda i, f: (0, f)), pl.BlockSpec((tf, d), lambda i, f: (f, 0))],
        out_specs=pl.BlockSpec((tm, d), lambda i, f: (i, 0)),
        out_shape=jax.ShapeDtypeStruct((m, d), F32),
        scratch_shapes=[pltpu.VMEM((tm, d), BF16), pltpu.VMEM((tm, d), F32)],
        compiler_params=_params("parallel", "arbitrary"),
        name="mlp",
    )(x, g.reshape(1, d), wup, wdn)


def _top_blocks(scores, valid, n_sel, axis):
    idxf = lax.broadcasted_iota(jnp.int32, scores.shape, axis).astype(F32)
    sv = jnp.where(valid, scores, -jnp.inf)
    sel = jnp.zeros(scores.shape, F32)
    for _ in range(n_sel):
        mx = jnp.max(sv, axis=axis, keepdims=True)
        idx = jnp.min(jnp.where(sv == mx, idxf, float(LANES)), axis=axis, keepdims=True)
        pick = idxf == idx
        sel = jnp.where(pick, 1.0, sel)
        sv = jnp.where(pick, -jnp.inf, sv)
    return jnp.where(valid, sel, 0.0)


def _moba_prompt_kernel(q_ref, k_ref, v_ref, onehot_ref, o_ref, *, n_blocks, heads_per_step):
    scale = HEAD_DIM ** -0.5
    blk = MOBA_BLOCK
    seq = q_ref.shape[2]
    row = lax.broadcasted_iota(jnp.int32, (blk, blk), 0)
    col = lax.broadcasted_iota(jnp.int32, (blk, blk), 1)
    eye = jnp.where(row == col, 1.0, 0.0).astype(BF16)

    heads = []
    for hp in range(heads_per_step):
        q = q_ref[0, hp]
        k = k_ref[0, hp]
        kmean = jnp.concatenate(
            [jnp.sum(k[j * blk:(j + 1) * blk], axis=0, keepdims=True) / blk for j in range(n_blocks)], axis=0)
        gate_t = lax.dot_general(kmean, q, NT_DIMS, precision=HIGHEST, preferred_element_type=F32)
        key_blk = lax.broadcasted_iota(jnp.int32, gate_t.shape, 0)
        qry_blk = lax.broadcasted_iota(jnp.int32, gate_t.shape, 1) // blk
        past = key_blk < qry_blk
        sel_t = _top_blocks(gate_t, past, min(MOBA_TOPK, n_blocks - 1), axis=0)
        allowed = (sel_t > 0.5) | (key_blk == qry_blk)
        mask_t = jnp.where(allowed, 0.0, MASK_VALUE)
        mask_t = jnp.concatenate([mask_t, jnp.zeros((LANES - n_blocks, seq), F32)], axis=0).astype(BF16)
        k_aug = jnp.concatenate([k.astype(BF16), onehot_ref[...]], axis=1)
        heads.append((q, mask_t, k_aug, v_ref[0, hp].astype(BF16)))

    for i in range(n_blocks):
        lo, hi = i * blk, (i + 1) * blk
        for hp, (q, mask_t, k_aug, v_b) in enumerate(heads):
            mask_i = lax.dot_general(eye, mask_t[:, lo:hi], NT_DIMS, preferred_element_type=F32)
            q_aug = jnp.concatenate([q[lo:hi].astype(BF16), mask_i.astype(BF16)], axis=1)
            s = lax.dot_general(q_aug, k_aug[0:hi], NT_DIMS, preferred_element_type=F32) * (scale * LOG2_E)
            s_own = jnp.where(col <= row, s[:, lo:hi], MASK_VALUE)
            m = jnp.max(s_own, axis=-1, keepdims=True)
            if i > 0:
                s_past = s[:, 0:lo]
                m = jnp.maximum(m, jnp.max(s_past, axis=-1, keepdims=True))
                p_past = jnp.exp2(s_past - m)
            p_own = jnp.exp2(s_own - m)
            l = jnp.sum(p_own, axis=-1, keepdims=True)
            o = jnp.dot(p_own.astype(BF16), v_b[lo:hi], preferred_element_type=F32)
            if i > 0:
                l = l + jnp.sum(p_past, axis=-1, keepdims=True)
                o = o + jnp.dot(p_past.astype(BF16), v_b[0:lo], preferred_element_type=F32)
            o_ref[0, lo:hi, hp * HEAD_DIM:(hp + 1) * HEAD_DIM] = (o / l).astype(o_ref.dtype)


MOBA_HEADS_PER_STEP = 1


def _moba_prompt(q, k, v):
    b, h, s, d = q.shape
    assert s % MOBA_BLOCK == 0 and d == HEAD_DIM
    nb = s // MOBA_BLOCK
    hps = MOBA_HEADS_PER_STEP
    assert nb % 8 == 0 and nb <= LANES and h % hps == 0
    onehot = jnp.asarray(np.arange(s)[:, None] // MOBA_BLOCK == np.arange(LANES)[None, :], BF16)
    head_map = lambda bi, hi: (bi, hi, 0, 0)
    return pl.pallas_call(
        functools.partial(_moba_prompt_kernel, n_blocks=nb, heads_per_step=hps),
        grid=(b, h // hps),
        in_specs=[pl.BlockSpec((1, hps, s, d), head_map), pl.BlockSpec((1, hps, s, d), head_map),
                  pl.BlockSpec((1, hps, s, d), head_map), pl.BlockSpec((s, LANES), lambda bi, hi: (0, 0))],
        out_specs=pl.BlockSpec((1, s, hps * d), lambda bi, hi: (bi, 0, hi)),
        out_shape=jax.ShapeDtypeStruct((b, s, h * d), BF16),
        compiler_params=_params("parallel", "parallel"),
        name="moba_prompt",
    )(q, k, v, onehot)


PAGE_GROUP = 16


def _own_head_lanes(x, n_heads, rows):
    lane_head = lax.broadcasted_iota(jnp.int32, (rows, LANES), 1) // (LANES // n_heads)
    out = x[0:rows]
    for h in range(1, n_heads):
        out = jnp.where(lane_head == h, x[h * rows:(h + 1) * rows], out)
    return out


def _moba_sample_probs_kernel(pt_ref, q_ref, kn_ref, *rest, n_pages, pages_per_block, group):
    kc_refs = rest[:group]
    p_ref, pown_ref, inv_ref, lt_ref, bsum_ref = rest[group:]
    step = pl.program_id(1)
    n_heads, n_tok, d = q_ref.shape[1:]
    page = kc_refs[0].shape[3]
    n_blocks = n_pages // pages_per_block
    scale = HEAD_DIM ** -0.5
    q_all = q_ref[0].reshape(n_heads * n_tok, d)
    q_all_b = q_all.astype(BF16)

    page_sums = []
    for g in range(group):
        keys = kc_refs[g][0, 0]
        page_sums.append(jnp.sum(keys, axis=1))
        full = lax.dot_general(keys.reshape(n_heads * page, d).astype(BF16), q_all_b, NT_DIMS,
                               preferred_element_type=F32)
        lt_ref[step * group + g] = _own_head_lanes(full, n_heads, page) * scale
    for bb in range(group // pages_per_block):
        tot = page_sums[bb * pages_per_block]
        for pp in range(1, pages_per_block):
            tot = tot + page_sums[bb * pages_per_block + pp]
        bsum_ref[step * (group // pages_per_block) + bb] = tot

    @pl.when(step == pl.num_programs(1) - 1)
    def _():
        lane = lax.broadcasted_iota(jnp.int32, (LANES, LANES), 1)
        row = lax.broadcasted_iota(jnp.int32, (LANES, LANES), 0)
        kmean = [jnp.concatenate([bsum_ref[b][h:h + 1] for b in range(n_blocks)], axis=0) / MOBA_BLOCK
                 for h in range(n_heads)]
        gate_full = lax.dot_general(jnp.concatenate(kmean, axis=0), q_all, NT_DIMS, precision=HIGHEST,
                                    preferred_element_type=F32)
        gate_t = _own_head_lanes(gate_full, n_heads, n_blocks)
        sel_t = _top_blocks(gate_t, row[0:n_blocks] >= 0, min(MOBA_TOPK, n_blocks), axis=0)
        kn_all = kn_ref[0].reshape(n_heads * n_tok, d).astype(BF16)
        own_full = lax.dot_general(kn_all, q_all_b, NT_DIMS, preferred_element_type=F32)
        own_t = _own_head_lanes(own_full, n_heads, n_tok) * scale
        own_t = jnp.where(row[0:n_tok] <= lane[0:n_tok] % n_tok, own_t, MASK_VALUE)

        picked = [sel_t[b:b + 1] > 0.5 for b in range(n_blocks)]
        mrun = jnp.full((page, LANES), MASK_VALUE, F32)
        for pg in range(n_pages):
            mrun = jnp.maximum(mrun, jnp.where(picked[pg // pages_per_block], lt_ref[pg], MASK_VALUE))
        m = jnp.maximum(jnp.max(mrun, axis=0, keepdims=True), jnp.max(own_t, axis=0, keepdims=True))
        e_own = jnp.exp(own_t - m)
        lrun = jnp.zeros((page, LANES), F32)
        for pg in range(n_pages):
            e = jnp.where(picked[pg // pages_per_block], jnp.exp(lt_ref[pg] - m), 0.0)
            lrun = lrun + e
            p_ref[0, pg] = e.T.reshape(n_heads, n_tok, page)
        total = jnp.sum(lrun, axis=0, keepdims=True) + jnp.sum(e_own, axis=0, keepdims=True)
        e_own_pad = jnp.concatenate([e_own, jnp.zeros((LANES - n_tok, LANES), F32)], axis=0)
        pown_ref[0] = e_own_pad.T.reshape(n_heads, n_tok, LANES)
        inv_ref[0] = jnp.broadcast_to(1.0 / total, (LANES, LANES)).T.reshape(n_heads, n_tok, LANES)


def _moba_sample_values_kernel(pt_ref, p_ref, pown_ref, inv_ref, vn_ref, *rest, group):
    vc_refs = rest[:group]
    o_ref = rest[group]
    step = pl.program_id(1)
    n_heads = vn_ref.shape[1]
    n_tok = vn_ref.shape[2]
    head = lambda h: slice(h * HEAD_DIM, (h + 1) * HEAD_DIM)

    @pl.when(step == 0)
    def _():
        for h in range(n_heads):
            pown = pown_ref[0, h]
            vn = vn_ref[0, h]
            own = pown[:, 0:1] * vn[0:1, :]
            for t in range(1, n_tok):
                own = own + pown[:, t:t + 1] * vn[t:t + 1, :]
            o_ref[0, :, head(h)] = own

    for h in range(n_heads):
        w = jnp.concatenate([p_ref[0, g, h] for g in range(group)], axis=1).astype(BF16)
        vals = jnp.concatenate([vc_refs[g][0, 0, h] for g in range(group)], axis=0).astype(BF16)
        o_ref[0, :, head(h)] += jnp.dot(w, vals, preferred_element_type=F32)

    @pl.when(step == pl.num_programs(1) - 1)
    def _():
        for h in range(n_heads):
            o_ref[0, :, head(h)] = o_ref[0, :, head(h)] * inv_ref[0, h]


def _moba_sample(q, k_new, v_new, cache_k, cache_v, page_table, layer):
    db, h, t, d = q.shape
    n_pages = page_table.shape[1]
    page = cache_k.shape[3]
    ppb = MOBA_BLOCK // page
    group = PAGE_GROUP
    assert page == LANES and d == HEAD_DIM and MOBA_BLOCK % page == 0
    assert (n_pages * page) % MOBA_BLOCK == 0 and t <= MOBA_BLOCK
    assert n_pages // ppb <= LANES and n_pages // ppb >= MOBA_TOPK
    assert n_pages % group == 0 and group % ppb == 0 and h * t == LANES and (n_pages // ppb) % 8 == 0
    n_steps = n_pages // group
    pt = page_table.reshape(-1).astype(jnp.int32)

    seq4 = lambda di, si, pt_ref: (di, 0, 0, 0)
    cache_specs = [
        pl.BlockSpec((1, 1, h, page, d),
                     lambda di, si, pt_ref, g=g: (layer, pt_ref[di * n_pages + si * group + g], 0, 0, 0))
        for g in range(group)]

    probs, p_own, inv = pl.pallas_call(
        functools.partial(_moba_sample_probs_kernel, n_pages=n_pages, pages_per_block=ppb, group=group),
        grid_spec=pltpu.PrefetchScalarGridSpec(
            num_scalar_prefetch=1,
            grid=(db, n_steps),
            in_specs=[pl.BlockSpec((1, h, t, d), seq4), pl.BlockSpec((1, h, t, d), seq4)] + cache_specs,
            out_specs=[pl.BlockSpec((1, n_pages, h, t, page), lambda di, si, pt_ref: (di, 0, 0, 0, 0)),
                       pl.BlockSpec((1, h, t, LANES), seq4), pl.BlockSpec((1, h, t, LANES), seq4)],
            scratch_shapes=[pltpu.VMEM((n_pages, page, LANES), F32), pltpu.VMEM((n_pages // ppb, h, d), F32)],
        ),
        out_shape=[jax.ShapeDtypeStruct((db, n_pages, h, t, page), F32),
                   jax.ShapeDtypeStruct((db, h, t, LANES), F32),
                   jax.ShapeDtypeStruct((db, h, t, LANES), F32)],
        compiler_params=_params("parallel", "arbitrary"),
        name="moba_sample_probs",
    )(pt, q, k_new, *([cache_k] * group))

    return pl.pallas_call(
        functools.partial(_moba_sample_values_kernel, group=group),
        grid_spec=pltpu.PrefetchScalarGridSpec(
            num_scalar_prefetch=1,
            grid=(db, n_steps),
            in_specs=[pl.BlockSpec((1, group, h, t, page), lambda di, si, pt_ref: (di, si, 0, 0, 0)),
                      pl.BlockSpec((1, h, t, LANES), seq4), pl.BlockSpec((1, h, t, LANES), seq4),
                      pl.BlockSpec((1, h, t, d), seq4)] + cache_specs,
            out_specs=pl.BlockSpec((1, t, h * d), lambda di, si, pt_ref: (di, 0, 0)),
        ),
        out_shape=jax.ShapeDtypeStruct((db, t, h * d), F32),
        compiler_params=_params("parallel", "arbitrary"),
        name="moba_sample_values",
    )(pt, probs, p_own, inv, v_new, *([cache_v] * group))


def _gla_tables(c, key_rows):
    widths = []
    w = c
    while w >= 2:
        widths.append(w)
        w //= 2
    t = np.arange(c)
    prefix = [np.tril(np.ones((c, c), np.float32))]
    masks = [np.eye(c, dtype=np.float32)]
    for w in widths:
        base = (t // w) * w
        mid = base + w // 2
        upper = t >= mid
        pm = np.zeros((c, c), np.float32)
        for r in range(c):
            if upper[r]:
                pm[r, mid[r]:r + 1] = 1.0
            else:
                pm[r, r + 1:mid[r]] = 1.0
        prefix.append(pm)
        same = base[:, None] == base[None, :]
        masks.append((same & upper[:, None] & (~upper)[None, :]).astype(np.float32))
    pad = ((0, 0), (0, key_rows - c))
    return np.pad(np.concatenate(prefix, axis=0), pad), np.pad(np.stack(masks, axis=0), ((0, 0),) + pad)


def _pad_rows(x, n):
    if x.shape[0] == n:
        return x
    return jnp.concatenate([x, jnp.zeros((n - x.shape[0], x.shape[1]), x.dtype)], axis=0)


def _gla_chunk(q, k, v_f32, a, pre, msk_ref, state_t):
    n_levels, c, kr = msk_ref.shape
    v = v_f32.astype(BF16)
    a1 = a.astype(BF16)
    a2 = (a - a1.astype(F32)).astype(BF16)
    sums = jnp.dot(pre, a1, preferred_element_type=F32) + jnp.dot(pre, a2, preferred_element_type=F32)

    cum = sums[0:c]
    o = lax.dot_general((q * jnp.exp(cum)).astype(BF16), state_t.astype(BF16), NT_DIMS,
                        preferred_element_type=F32)
    scores = lax.dot_general(q.astype(BF16), _pad_rows(k, kr).astype(BF16), NT_DIMS,
                             preferred_element_type=F32) * msk_ref[0]
    for lv in range(1, n_levels):
        e = jnp.exp(sums[lv * c:(lv + 1) * c])
        s_lv = lax.dot_general((q * e).astype(BF16), _pad_rows(k * e, kr).astype(BF16), NT_DIMS,
                               preferred_element_type=F32)
        scores = scores + s_lv * msk_ref[lv]
    o = o + jnp.dot(scores.astype(BF16), v, preferred_element_type=F32)

    last = cum[c - 1:c]
    k_dec = _pad_rows(k * jnp.exp(last - cum), kr).astype(BF16)
    upd_t = jnp.dot(v_f32.T.astype(BF16), k_dec, preferred_element_type=F32)
    return o, jnp.exp(last) * state_t + upd_t


def _gla_kernel(*refs, has_state0, heads_per_step):
    if has_state0:
        q_ref, k_ref, v_ref, og_ref, a_ref, pre_ref, msk_ref, g_ref, s0_ref, y_ref, sout_ref, st_ref = refs
    else:
        q_ref, k_ref, v_ref, og_ref, a_ref, pre_ref, msk_ref, g_ref, y_ref, sout_ref, st_ref = refs
        s0_ref = None
    ci = pl.program_id(2)
    _, c, kr = msk_ref.shape
    n_rows = q_ref.shape[0]
    gk, gv = GLA_KEY_DIM, GLA_VAL_DIM

    @pl.when(ci == 0)
    def _():
        for hp in range(heads_per_step):
            if has_state0:
                st_ref[hp] = s0_ref[0, hp].T
            else:
                st_ref[hp] = jnp.zeros(st_ref.shape[1:], F32)

    pre = pre_ref[...]
    for hp in range(heads_per_step):
        kcols = slice(hp * gk, (hp + 1) * gk)
        vcols = slice(hp * gv, (hp + 1) * gv)
        q = _pad_rows(q_ref[:, kcols], c) * (GLA_KEY_DIM ** -0.5)
        k = _pad_rows(k_ref[:, kcols], c)
        v_f32 = _pad_rows(v_ref[:, vcols], kr)
        a = _pad_rows(a_ref[:, kcols], kr)
        o, new_state_t = _gla_chunk(q, k, v_f32, a, pre, msk_ref, st_ref[hp])
        st_ref[hp] = new_state_t

        @pl.when(ci == pl.num_programs(2) - 1)
        def _(hp=hp, new_state_t=new_state_t):
            sout_ref[0, hp] = new_state_t.T

        o = o[0:n_rows]
        ms = jnp.mean(o * o, axis=-1, keepdims=True)
        og = og_ref[:, vcols]
        y = (o * lax.rsqrt(ms + NORM_EPS) * g_ref[...]) * (og * _sigmoid(og))
        y_ref[:, vcols] = y.astype(y_ref.dtype)


GLA_HEADS_PER_STEP = 4


def _gla(u, log_a, gla_norm_g, state0, nseq, seqlen):
    m = u.shape[0]
    n_rows = min(GLA_CHUNK, seqlen)
    assert seqlen % n_rows == 0 and n_rows % 8 == 0
    n_chunks = seqlen // n_rows
    c = max(GLA_SHORT_CHUNK, n_rows)
    assert c & (c - 1) == 0 and c <= GLA_CHUNK
    pre_np, msk_np = _gla_tables(c, GLA_CHUNK)
    pre = jnp.asarray(pre_np, BF16)
    msk = jnp.asarray(msk_np, F32)
    gh, gk, gv = GLA_HEADS, GLA_KEY_DIM, GLA_VAL_DIM
    hps = GLA_HEADS_PER_STEP
    assert gh % hps == 0
    ng = gh // hps
    kw, vw = hps * gk, hps * gv
    k0, v0, og0 = (gh * gk) // kw, (2 * gh * gk) // vw, (2 * gh * gk + gh * gv) // vw
    assert k0 * kw == gh * gk and v0 * vw == 2 * gh * gk and og0 * vw == 2 * gh * gk + gh * gv
    row = lambda b, ci: b * n_chunks + ci
    in_specs = [
        pl.BlockSpec((n_rows, kw), lambda b, hg, ci: (row(b, ci), hg)),
        pl.BlockSpec((n_rows, kw), lambda b, hg, ci: (row(b, ci), k0 + hg)),
        pl.BlockSpec((n_rows, vw), lambda b, hg, ci: (row(b, ci), v0 + hg)),
        pl.BlockSpec((n_rows, vw), lambda b, hg, ci: (row(b, ci), og0 + hg)),
        pl.BlockSpec((n_rows, kw), lambda b, hg, ci: (row(b, ci), hg)),
        pl.BlockSpec(pre.shape, lambda b, hg, ci: (0, 0)),
        pl.BlockSpec(msk.shape, lambda b, hg, ci: (0, 0, 0)),
        pl.BlockSpec((1, gv), lambda b, hg, ci: (0, 0)),
    ]
    args = [u, u, u, u, log_a, pre, msk, gla_norm_g.reshape(1, gv)]
    if state0 is not None:
        in_specs.append(pl.BlockSpec((1, hps, gk, gv), lambda b, hg, ci: (b, hg, 0, 0)))
        args.append(state0)
    kern = functools.partial(_gla_kernel, has_state0=state0 is not None, heads_per_step=hps)
    return pl.pallas_call(
        kern,
        grid=(nseq, ng, n_chunks),
        in_specs=in_specs,
        out_specs=[pl.BlockSpec((n_rows, vw), lambda b, hg, ci: (row(b, ci), hg)),
                   pl.BlockSpec((1, hps, gk, gv), lambda b, hg, ci: (b, hg, 0, 0))],
        out_shape=[jax.ShapeDtypeStruct((m, gh * gv), BF16 if n_rows % 16 == 0 else F32),
                   jax.ShapeDtypeStruct((nseq, gh, gk, gv), F32)],
        scratch_shapes=[pltpu.VMEM((hps, gv, gk), F32)],
        compiler_params=_params("parallel", "parallel", "arbitrary"),
        name="gla",
    )(*args)


ROW_TILE = 512
WIDE_ROW_TILE = 1024
COL_TILE = 1024
MERGE_COL_TILE = 512


def _tile(m, pref):
    t = min(m, pref)
    assert m % t == 0
    return t


def _run_group(x, attend, state0, wts):
    nseq, seqlen, d = x.shape
    m = nseq * seqlen
    x2 = x.reshape(m, d)
    tm = _tile(m, ROW_TILE)
    tm_wide = _tile(m, WIDE_ROW_TILE)

    h = _rmsnorm(x2, wts["norm1_g"], tm)
    tmh = _tile(seqlen, WIDE_ROW_TILE) if seqlen >= 256 else m
    w_in, aw, nh = wts["w_in"], ATTN_HEADS * HEAD_DIM, ATTN_HEADS
    hpt = COL_TILE // HEAD_DIM
    q = _proj_heads(h, w_in, 0, nh, wts["q_norm_g"], nseq, seqlen, True, tmh, hpt, "proj_q")
    k = _proj_heads(h, w_in, aw, nh, wts["k_norm_g"], nseq, seqlen, True, tmh, hpt, "proj_k")
    v = _proj_heads(h, w_in, 2 * aw, nh, wts["k_norm_g"], nseq, seqlen, False, tmh, hpt, "proj_v")
    gla_cols = 2 * GLA_HEADS * (GLA_KEY_DIM + GLA_VAL_DIM)
    u = _matmul(h, w_in, 3 * aw, gla_cols, F32, tm_wide, COL_TILE, "proj_gla")
    log_a = _loga(h, wts["w_ag"], wts["w_a2"], wts["b_a"], tm)

    y_attn = attend(q, k, v)
    y_gla, gla_state = _gla(u, log_a, wts["gla_norm_g"], state0, nseq, seqlen)

    mixed = _merge(h, y_attn.reshape(m, -1), y_gla, wts["w_gates"],
                   wts["w_branch_attn"], wts["w_branch_gla"], tm, MERGE_COL_TILE)
    x1 = _matmul_residual(mixed, wts["w_out"], x2, tm_wide, COL_TILE, "out_proj")
    y = _mlp(x1, wts["norm2_g"], wts["w_up"], wts["w_down"], tm, COL_TILE)
    return y.reshape(nseq, seqlen, d), k, v, gla_state


def kernel(x_prompt, x_sample, cache_k, cache_v, state_gla, page_table, norm1_g, w_in, q_norm_g, k_norm_g,
           w_a2, b_a, gla_norm_g, w_branch_attn, w_branch_gla, w_out, norm2_g, w_up, w_down):
    depth = w_in.shape[0]
    d_model = x_prompt.shape[-1]
    ag0 = 3 * ATTN_HEADS * HEAD_DIM + 2 * GLA_HEADS * (GLA_KEY_DIM + GLA_VAL_DIM)

    yp, ys = x_prompt, x_sample
    outs = [[] for _ in range(6)]
    for l in range(depth):
        w_l = w_in[l].astype(BF16)
        assert w_l.shape[1] == ag0 + GLA_GATE_RANK + 2 * d_model
        wts = {
            "norm1_g": norm1_g[l], "q_norm_g": q_norm_g[l], "k_norm_g": k_norm_g[l],
            "w_in": w_l,
            "w_ag": jnp.pad(w_l[:, ag0:ag0 + GLA_GATE_RANK], ((0, 0), (0, LANES - GLA_GATE_RANK))),
            "w_gates": w_l[:, ag0 + GLA_GATE_RANK:],
            "w_a2": jnp.pad(w_a2[l], ((0, LANES - GLA_GATE_RANK), (0, 0))),
            "b_a": b_a[l].reshape(1, -1), "gla_norm_g": gla_norm_g[l],
            "w_branch_attn": w_branch_attn[l].astype(BF16), "w_branch_gla": w_branch_gla[l].astype(BF16),
            "w_out": w_out[l].astype(BF16), "norm2_g": norm2_g[l],
            "w_up": w_up[l].astype(BF16), "w_down": w_down[l].astype(BF16),
        }
        yp, k_l, v_l, s_l = _run_group(yp, _moba_prompt, None, wts)
        outs[0].append(k_l); outs[1].append(v_l); outs[2].append(s_l)
        attend_s = lambda q, k, v: _moba_sample(q, k, v, cache_k, cache_v, page_table, l)
        ys, k_l, v_l, s_l = _run_group(ys, attend_s, state_gla[l], wts)
        outs[3].append(k_l); outs[4].append(v_l); outs[5].append(s_l)
    stack = lambda xs: xs[0][None] if len(xs) == 1 else jnp.stack(xs)
    return (yp, ys) + tuple(stack(o) for o in outs)
```
